```python
import math
import jax, jax.numpy as jnp
from jax import lax
import numpy as np

D_MODEL = 1024
BATCH = 8
SEQ = 4096
DEPTH = 2

N_MIXERS = 2
N_MLA_LAYERS = (DEPTH + N_MIXERS - 1) // N_MIXERS
N_HYENA_LAYERS = DEPTH // N_MIXERS

MLA_HEADS = 8
Q_LORA = 256
KV_LORA = 128
QK_NOPE = 128
QK_ROPE = 64
QK_HEAD = QK_NOPE + QK_ROPE
V_HEAD = 128
ROPE_HALF = QK_ROPE // 2
ROPE_THETA = 10000.0
Q_BLOCK = 128

HYENA_ORDER = 2
SHORT_CONV = 3
FILTER_EMB = 33
FILTER_BANDS = (FILTER_EMB - 1) // 2
FILTER_WIDTH = 64
FAST_DECAY = 0.3
SLOW_DECAY = 1.5
DECAY_TARGET = 1e-2

N_GROUPS = 4
EXPERTS_PER_GROUP = 4
N_EXPERTS = N_GROUPS * EXPERTS_PER_GROUP
TOP_K = 2
D_EXPERT = 256

EPS = 1e-6

kernel_name = "hybrid_mla_hyena_hmoe_encoder"


def rms_norm(x, g):
    xf = x.astype(jnp.float32)
    y = xf * lax.rsqrt(jnp.mean(xf * xf, axis=-1, keepdims=True) + EPS)
    return (y * g.astype(jnp.float32)).astype(x.dtype)


def modulate(h, shift, scale):
    return h * (1 + scale[:, None, :]) + shift[:, None, :]


def rope_tables(positions):
    inv_freq = 1.0 / (ROPE_THETA ** (jnp.arange(0, QK_ROPE, 2, dtype=jnp.float32) / QK_ROPE))
    ang = positions.astype(jnp.float32)[..., None] * inv_freq
    return jnp.cos(ang), jnp.sin(ang)


def apply_rope(x, cos, sin):
    xf = x.astype(jnp.float32)
    x1, x2 = xf[..., :ROPE_HALF], xf[..., ROPE_HALF:]
    return jnp.concatenate([x1 * cos - x2 * sin, x1 * sin + x2 * cos], axis=-1).astype(x.dtype)


def mla_mixer(h, cos, sin, w_down, q_a_g, kv_a_g, w_uq, w_ukv, q_norm_g, k_norm_g, w_o):
    B, S, _ = h.shape
    lat = h @ w_down
    c_q = lat[..., :Q_LORA]
    c_kv = lat[..., Q_LORA:Q_LORA + KV_LORA]
    k_pe = lat[..., Q_LORA + KV_LORA:]
    q = (rms_norm(c_q, q_a_g) @ w_uq).reshape(B, S, MLA_HEADS, QK_HEAD)
    kv = (rms_norm(c_kv, kv_a_g) @ w_ukv).reshape(B, S, MLA_HEADS, QK_NOPE + V_HEAD)
    cos_h, sin_h = cos[:, :, None, :], sin[:, :, None, :]
    q_nope = rms_norm(q[..., :QK_NOPE], q_norm_g[:QK_NOPE])
    q_pe = apply_rope(rms_norm(q[..., QK_NOPE:], q_norm_g[QK_NOPE:]), cos_h, sin_h)
    k_nope = rms_norm(kv[..., :QK_NOPE], k_norm_g[:QK_NOPE])
    k_pe = apply_rope(rms_norm(k_pe, k_norm_g[QK_NOPE:]), cos, sin)
    v = kv[..., QK_NOPE:]
    q = jnp.concatenate([q_nope, q_pe], axis=-1)
    k = jnp.concatenate([k_nope, jnp.broadcast_to(k_pe[:, :, None, :], (B, S, MLA_HEADS, QK_ROPE))], axis=-1)
    scale = QK_HEAD ** -0.5
    n_blk = S // Q_BLOCK
    q_blocks = q.reshape(B, n_blk, Q_BLOCK, MLA_HEADS, QK_HEAD).transpose(1, 0, 2, 3, 4)

    def attend(q_blk):
        s = jnp.einsum('bqhd,bkhd->bhqk', q_blk, k, preferred_element_type=jnp.float32) * scale
        p = jax.nn.softmax(s, axis=-1).astype(v.dtype)
        return jnp.einsum('bhqk,bkhd->bqhd', p, v)

    o = lax.map(attend, q_blocks)
    o = o.transpose(1, 0, 2, 3, 4).reshape(B, S, MLA_HEADS * V_HEAD)
    return o @ w_o


def hyena_filters(L, w1, b1, f1, w2, b2, f2, w3, b3, f3, w4):
    f32 = jnp.float32
    t = jnp.linspace(0.0, 1.0, L, dtype=f32)[:, None]
    w = 2.0 * math.pi * jnp.arange(L, dtype=f32)[:, None] / L
    fr = jnp.linspace(1e-4, FILTER_BANDS - 1, FILTER_BANDS, dtype=f32)[None, :]
    z = jnp.concatenate([t, jnp.cos(fr * w), -jnp.sin(fr * w)], axis=-1)
    hdn = jnp.sin(f1.astype(f32) * (z @ w1.astype(f32) + b1.astype(f32)))
    hdn = jnp.sin(f2.astype(f32) * (hdn @ w2.astype(f32) + b2.astype(f32)))
    hdn = jnp.sin(f3.astype(f32) * (hdn @ w3.astype(f32) + b3.astype(f32)))
    h = (hdn @ w4.astype(f32)).reshape(L, HYENA_ORDER, 2, D_MODEL)
    deltas = jnp.abs(jnp.linspace(math.log(FAST_DECAY) / DECAY_TARGET,
                                  math.log(SLOW_DECAY) / DECAY_TARGET, D_MODEL, dtype=f32))
    decay = jnp.exp(-t * deltas)
    h = h * decay[:, None, None, :]
    return h / (jnp.sum(jnp.abs(h), axis=0, keepdims=True) + EPS)


def bidir_long_conv(u, h_fwd, h_bwd, bias):
    B, L, D = u.shape
    k2 = jnp.concatenate([h_fwd, jnp.zeros((1, D), jnp.float32), h_bwd[:0:-1]], axis=0)
    k_f = jnp.fft.rfft(k2, axis=0)
    u_f = jnp.fft.rfft(u.astype(jnp.float32), n=2 * L, axis=1)
    y = jnp.fft.irfft(u_f * k_f[None], n=2 * L, axis=1)[:, :L]
    return (y + u.astype(jnp.float32) * bias.astype(jnp.float32)).astype(u.dtype)


def hyena_mixer(h, w_in, b_in, conv_w, conv_b, f_w1, f_b1, f_freq1, f_w2, f_b2, f_freq2,
                f_w3, f_b3, f_freq3, f_w4, filt_bias, w_out):
    B, L, _ = h.shape
    C = 3 * D_MODEL
    u = h @ w_in + b_in
    u = lax.conv_general_dilated(u, conv_w[:, None, :].astype(u.dtype), window_strides=(1,),
                                 padding=((SHORT_CONV // 2, SHORT_CONV // 2),),
                                 dimension_numbers=('NWC', 'WIO', 'NWC'),
                                 feature_group_count=C) + conv_b
    x1, x2, v = jnp.split(u, 3, axis=-1)
    filt = hyena_filters(L, f_w1, f_b1, f_freq1, f_w2, f_b2, f_freq2, f_w3, f_b3, f_freq3, f_w4)
    z = v
    for o, gate in enumerate((x1, x2)):
        z = gate * bidir_long_conv(z, filt[:, o, 0], filt[:, o, 1], filt_bias[o])
    return z @ w_out


def hier_moe(h, wg, bg, we, be, w_gate, w_up, w_down):
    B, S, D = h.shape
    t = h.reshape(B * S, D)
    T = t.shape[0]
    g_prob = jax.nn.softmax((t @ wg).astype(jnp.float32) + bg.astype(jnp.float32), axis=-1)
    g_w, g_idx = lax.top_k(g_prob, 1)
    e_logits = ((t @ we).astype(jnp.float32) + be.astype(jnp.float32)).reshape(T, N_GROUPS, EXPERTS_PER_GROUP)
    e_logits = jnp.take_along_axis(e_logits, g_idx[:, :, None], axis=1)[:, 0]
    e_w, e_idx = lax.top_k(jax.nn.softmax(e_logits, axis=-1), TOP_K)
    weights = g_w * (e_w / jnp.sum(e_w, axis=-1, keepdims=True))
    expert = g_idx * EXPERTS_PER_GROUP + e_idx
    combine = jnp.sum(jax.nn.one_hot(expert, N_EXPERTS, dtype=jnp.float32) * weights[..., None], axis=1)
    combine = combine.astype(t.dtype)
    out = jnp.zeros_like(t)
    for e in range(N_EXPERTS):
        a = jax.nn.silu(t @ w_gate[e]) * (t @ w_up[e])
        out = out + combine[:, e:e + 1] * (a @ w_down[e])
    return out.reshape(B, S, D)


def setup_inputs(seed: int = 0) -> dict:
    key = jax.random.key(seed)
    ks = iter(jax.random.split(key, 64))
    D = D_MODEL
    NA, NH = N_MLA_LAYERS, N_HYENA_LAYERS

    def nrm(shape, scale):
        return jax.random.normal(next(ks), shape, jnp.float32) * scale

    def gain(shape):
        return 1.0 + nrm(shape, 0.02)

    x = nrm((BATCH, SEQ, D), 1.0)
    c = nrm((BATCH, D), 1.0)
    offs = jax.random.randint(next(ks), (BATCH, 1), 0, 1024, dtype=jnp.int32)
    positions = offs + jnp.arange(SEQ, dtype=jnp.int32)[None, :]
    return {
        "x": x, "c": c, "positions": positions,
        "ada_w": nrm((DEPTH, D, 6 * D), 0.5 * D ** -0.5),
        "ada_b": nrm((DEPTH, 6 * D), 0.02),
        "norm_mix_g": gain((DEPTH, D)),
        "norm_ffn_g": gain((DEPTH, D)),
        "mla_w_down": nrm((NA, D, Q_LORA + KV_LORA + QK_ROPE), D ** -0.5),
        "mla_q_a_g": gain((NA, Q_LORA)),
        "mla_kv_a_g": gain((NA, KV_LORA)),
        "mla_w_uq": nrm((NA, Q_LORA, MLA_HEADS * QK_HEAD), Q_LORA ** -0.5),
        "mla_w_ukv": nrm((NA, KV_LORA, MLA_HEADS * (QK_NOPE + V_HEAD)), KV_LORA ** -0.5),
        "mla_q_norm_g": gain((NA, QK_HEAD)),
        "mla_k_norm_g": gain((NA, QK_HEAD)),
        "mla_w_o": nrm((NA, MLA_HEADS * V_HEAD, D), (MLA_HEADS * V_HEAD) ** -0.5),
        "hy_w_in": nrm((NH, D, 3 * D), D ** -0.5),
        "hy_b_in": nrm((NH, 3 * D), 0.02),
        "hy_conv_w": nrm((NH, SHORT_CONV, 3 * D), SHORT_CONV ** -0.5),
        "hy_conv_b": nrm((NH, 3 * D), 0.02),
        "hy_f_w1": nrm((NH, FILTER_EMB, FILTER_WIDTH), FILTER_EMB ** -0.5),
        "hy_f_b1": nrm((NH, FILTER_WIDTH), 0.1),
        "hy_f_freq1": gain((NH, FILTER_WIDTH)),
        "hy_f_w2": nrm((NH, FILTER_WIDTH, FILTER_WIDTH), FILTER_WIDTH ** -0.5),
        "hy_f_b2": nrm((NH, FILTER_WIDTH), 0.1),
        "hy_f_freq2": gain((NH, FILTER_WIDTH)),
        "hy_f_w3": nrm((NH, FILTER_WIDTH, FILTER_WIDTH), FILTER_WIDTH ** -0.5),
        "hy_f_b3": nrm((NH, FILTER_WIDTH), 0.1),
        "hy_f_freq3": gain((NH, FILTER_WIDTH)),
        "hy_f_w4": nrm((NH, FILTER_WIDTH, HYENA_ORDER * 2 * D), FILTER_WIDTH ** -0.5),
        "hy_filt_bias": nrm((NH, HYENA_ORDER, D), 1.0),
        "hy_w_out": nrm((NH, D, D), D ** -0.5),
        "moe_wg": nrm((DEPTH, D, N_GROUPS), D ** -0.5),
        "moe_bg": nrm((DEPTH, N_GROUPS), 0.01),
        "moe_we": nrm((DEPTH, D, N_EXPERTS), D ** -0.5),
        "moe_be": nrm((DEPTH, N_EXPERTS), 0.01),
        "moe_w_gate": nrm((DEPTH, N_EXPERTS, D, D_EXPERT), D ** -0.5),
        "moe_w_up": nrm((DEPTH, N_EXPERTS, D, D_EXPERT), D ** -0.5),
        "moe_w_down": nrm((DEPTH, N_EXPERTS, D_EXPERT, D), D_EXPERT ** -0.5),
    }


def reference(x, c, positions, ada_w, ada_b, norm_mix_g, norm_ffn_g,
              mla_w_down, mla_q_a_g, mla_kv_a_g, mla_w_uq, mla_w_ukv, mla_q_norm_g, mla_k_norm_g, mla_w_o,
              hy_w_in, hy_b_in, hy_conv_w, hy_conv_b,
              hy_f_w1, hy_f_b1, hy_f_freq1, hy_f_w2, hy_f_b2, hy_f_freq2,
              hy_f_w3, hy_f_b3, hy_f_freq3, hy_f_w4, hy_filt_bias, hy_w_out,
              moe_wg, moe_bg, moe_we, moe_be, moe_w_gate, moe_w_up, moe_w_down):
    cos, sin = rope_tables(positions)
    c_act = jax.nn.silu(c.astype(jnp.float32))
    for i in range(DEPTH):
        mod = (c_act @ ada_w[i].astype(jnp.float32) + ada_b[i].astype(jnp.float32)).astype(x.dtype)
        sh1, sc1, g1, sh2, sc2, g2 = jnp.split(mod, 6, axis=-1)
        h = modulate(rms_norm(x, norm_mix_g[i]), sh1, sc1)
        j = i // N_MIXERS
        if i % N_MIXERS == 0:
            y = mla_mixer(h, cos, sin, mla_w_down[j], mla_q_a_g[j], mla_kv_a_g[j], mla_w_uq[j],
                          mla_w_ukv[j], mla_q_norm_g[j], mla_k_norm_g[j], mla_w_o[j])
        else:
            y = hyena_mixer(h, hy_w_in[j], hy_b_in[j], hy_conv_w[j], hy_conv_b[j],
                            hy_f_w1[j], hy_f_b1[j], hy_f_freq1[j], hy_f_w2[j], hy_f_b2[j], hy_f_freq2[j],
                            hy_f_w3[j], hy_f_b3[j], hy_f_freq3[j], hy_f_w4[j], hy_filt_bias[j], hy_w_out[j])
        x = x + g1[:, None, :] * y
        h = modulate(rms_norm(x, norm_ffn_g[i]), sh2, sc2)
        x = x + g2[:, None, :] * hier_moe(h, moe_wg[i], moe_bg[i], moe_we[i], moe_be[i],
                                          moe_w_gate[i], moe_w_up[i], moe_w_down[i])
    return x
```

```python
import functools
import math

import jax
import jax.numpy as jnp
import numpy as np
from jax import lax
from jax.experimental import pallas as pl
from jax.experimental.pallas import tpu as pltpu

F32 = jnp.float32
BF16 = jnp.bfloat16
HIGHEST = lax.Precision.HIGHEST

D_MODEL = 1024
MLA_HEADS = 8
Q_LORA = 256
KV_LORA = 128
QK_NOPE = 128
QK_ROPE = 64
QK_HEAD = QK_NOPE + QK_ROPE
V_HEAD = 128
ROPE_HALF = QK_ROPE // 2
ROPE_THETA = 10000.0
HYENA_ORDER = 2
FILTER_EMB = 33
FILTER_BANDS = (FILTER_EMB - 1) // 2
FAST_DECAY = 0.3
SLOW_DECAY = 1.5
DECAY_TARGET = 1e-2
N_GROUPS = 4
EXPERTS_PER_GROUP = 4
N_EXPERTS = N_GROUPS * EXPERTS_PER_GROUP
D_EXPERT = 256
EPS = 1e-6

LANES = 128
QK_PAD = 2 * LANES
LAT_PAD = 512
CONV_BLOCKS = 4
VMEM_LIMIT = 56 * 1024 * 1024


def _cparams(*sem):
    return pltpu.CompilerParams(dimension_semantics=sem, vmem_limit_bytes=VMEM_LIMIT)


def _adaln_kernel(c_ref, w_ref, b_ref, o_ref):
    c = c_ref[...]
    ca = c * jax.nn.sigmoid(c)
    o_ref[0] = jnp.dot(ca, w_ref[0], precision=HIGHEST, preferred_element_type=F32) + b_ref[0]


def adaln(c, ada_w, ada_b, tn=1536):
    depth, d, n = ada_w.shape
    b = c.shape[0]
    return pl.pallas_call(
        _adaln_kernel,
        grid=(depth, n // tn),
        in_specs=[
            pl.BlockSpec((b, d), lambda i, j: (0, 0)),
            pl.BlockSpec((1, d, tn), lambda i, j: (i, 0, j)),
            pl.BlockSpec((1, 1, tn), lambda i, j: (i, 0, j)),
        ],
        out_specs=pl.BlockSpec((1, b, tn), lambda i, j: (i, 0, j)),
        out_shape=jax.ShapeDtypeStruct((depth, b, n), F32),
        compiler_params=_cparams("arbitrary", "arbitrary"),
        name="adaln",
    )(c, ada_w, ada_b.reshape(depth, 1, n))


def _norm_mod(x, g, sh, sc):
    ms = jnp.mean(x * x, axis=-1, keepdims=True)
    y = x * lax.rsqrt(ms + EPS) * g
    return y * (1.0 + sc) + sh


def _nmm_kernel(x_ref, g_ref, sh_ref, sc_ref, w_ref, b_ref, o_ref):
    h = _norm_mod(x_ref[...], g_ref[...], sh_ref[0], sc_ref[0])
    o = jnp.dot(h.astype(BF16), w_ref[...], preferred_element_type=F32) + b_ref[...]
    o_ref[...] = o.astype(o_ref.dtype)


def norm_mod_matmul(x, g, sh, sc, w, b, seq, out_dtype, tm=512, name="nmm"):
    t, d = x.shape
    n = w.shape[1]
    per = seq // tm
    nb = sh.shape[0]
    return pl.pallas_call(
        _nmm_kernel,
        grid=(t // tm,),
        in_specs=[
            pl.BlockSpec((tm, d), lambda i: (i, 0)),
            pl.BlockSpec((1, d), lambda i: (0, 0)),
            pl.BlockSpec((1, 1, d), lambda i: (i // per, 0, 0)),
            pl.BlockSpec((1, 1, d), lambda i: (i // per, 0, 0)),
            pl.BlockSpec((d, n), lambda i: (0, 0)),
            pl.BlockSpec((1, n), lambda i: (0, 0)),
        ],
        out_specs=pl.BlockSpec((tm, n), lambda i: (i, 0)),
        out_shape=jax.ShapeDtypeStruct((t, n), out_dtype),
        compiler_params=_cparams("arbitrary"),
        name=name,
    )(x, g.reshape(1, d), sh.reshape(nb, 1, d), sc.reshape(nb, 1, d), w, b.reshape(1, n))


def _mmres_kernel(a_ref, w_ref, x_ref, g_ref, o_ref):
    y = jnp.dot(a_ref[...], w_ref[...], preferred_element_type=F32)
    o_ref[...] = x_ref[...] + g_ref[0] * y


def matmul_residual(a, w, x, gate, seq, tm=512, name="mmres"):
    t, k = a.shape
    d = w.shape[1]
    per = seq // tm
    nb = gate.shape[0]
    return pl.pallas_call(
        _mmres_kernel,
        grid=(t // tm,),
        in_specs=[
            pl.BlockSpec((tm, k), lambda i: (i, 0)),
            pl.BlockSpec((k, d), lambda i: (0, 0)),
            pl.BlockSpec((tm, d), lambda i: (i, 0)),
            pl.BlockSpec((1, 1, d), lambda i: (i // per, 0, 0)),
        ],
        out_specs=pl.BlockSpec((tm, d), lambda i: (i, 0)),
        out_shape=jax.ShapeDtypeStruct((t, d), F32),
        compiler_params=_cparams("arbitrary"),
        name=name,
    )(a, w, x, gate.reshape(nb, 1, d))


def _rms(x, n):
    return x * lax.rsqrt(jnp.sum(x * x, axis=-1, keepdims=True) * (1.0 / n) + EPS)


def _qkv_kernel(lat_ref, cos_ref, sin_ref, qag_ref, kvag_ref, wq_ref, wk_ref, wv_ref,
                gq_ref, gkn_ref, gkp_ref, q_ref, k_ref, v_ref):
    lat = lat_ref[...]
    cq = lat[:, :Q_LORA]
    ckv = lat[:, Q_LORA:Q_LORA + KV_LORA]
    kpe = lat[:, Q_LORA + KV_LORA:]
    cqn = (_rms(cq, Q_LORA) * qag_ref[...]).astype(BF16)
    ckvn = (_rms(ckv, KV_LORA) * kvag_ref[...]).astype(BF16)
    qall = jnp.dot(cqn, wq_ref[...], preferred_element_type=F32)
    kall = jnp.dot(ckvn, wk_ref[...], preferred_element_type=F32)
    vall = jnp.dot(ckvn, wv_ref[...], preferred_element_type=F32)
    cos_t = cos_ref[...]
    sin_t = sin_ref[...]
    lane = lax.broadcasted_iota(jnp.int32, cos_t.shape, 1)

    def rope(pe):
        up = pltpu.roll(pe, ROPE_HALF, 1)
        down = pltpu.roll(pe, LANES - ROPE_HALF, 1)
        rot = jnp.where(lane < ROPE_HALF, -down, up)
        return pe * cos_t + rot * sin_t

    gq = gq_ref[...]
    kp = rope(_rms(kpe, QK_ROPE) * gkp_ref[...])
    scale = QK_HEAD ** -0.5
    for h in range(MLA_HEADS):
        qh = qall[:, h * QK_PAD:(h + 1) * QK_PAD]
        qn = _rms(qh[:, :QK_NOPE], QK_NOPE) * gq[:, :QK_NOPE]
        qp = rope(_rms(qh[:, QK_NOPE:], QK_ROPE) * gq[:, QK_NOPE:])
        q_ref[0, h] = (jnp.concatenate([qn, qp], axis=-1) * scale).astype(q_ref.dtype)
        kn = _rms(kall[:, h * QK_NOPE:(h + 1) * QK_NOPE], QK_NOPE) * gkn_ref[...]
        k_ref[0, h] = jnp.concatenate([kn, kp], axis=-1).astype(k_ref.dtype)
        v_ref[0, h] = vall[:, h * V_HEAD:(h + 1) * V_HEAD].astype(v_ref.dtype)


def mla_qkv(lat, cos_t, sin_t, q_a_g, kv_a_g, wq, wk, wv, gq, gkn, gkp, batch, seq, tm=256):
    t = lat.shape[0]
    per = seq // tm
    hh = MLA_HEADS
    full = lambda shape: pl.BlockSpec(shape, lambda i: (0,) * len(shape))
    return pl.pallas_call(
        _qkv_kernel,
        grid=(t // tm,),
        in_specs=[
            pl.BlockSpec((tm, LAT_PAD), lambda i: (i, 0)),
            pl.BlockSpec((tm, LANES), lambda i: (i, 0)),
            pl.BlockSpec((tm, LANES), lambda i: (i, 0)),
            full((1, Q_LORA)), full((1, KV_LORA)),
            full(wq.shape), full(wk.shape), full(wv.shape),
            full((1, QK_PAD)), full((1, LANES)), full((1, LANES)),
        ],
        out_specs=[
            pl.BlockSpec((1, hh, tm, QK_PAD), lambda i: (i // per, 0, i % per, 0)),
            pl.BlockSpec((1, hh, tm, QK_PAD), lambda i: (i // per, 0, i % per, 0)),
            pl.BlockSpec((1, hh, tm, V_HEAD), lambda i: (i // per, 0, i % per, 0)),
        ],
        out_shape=[
            jax.ShapeDtypeStruct((batch, hh, seq, QK_PAD), BF16),
            jax.ShapeDtypeStruct((batch, hh, seq, QK_PAD), BF16),
            jax.ShapeDtypeStruct((batch, hh, seq, V_HEAD), BF16),
        ],
        compiler_params=_cparams("arbitrary"),
        name="mla_qkv",
    )(lat, cos_t, sin_t, q_a_g, kv_a_g, wq, wk, wv, gq, gkn, gkp)


def _attn_kernel(q_ref, k_ref, v_ref, o_ref, *, tk):
    q = q_ref[0, 0]
    tq = q.shape[0]
    nk = k_ref.shape[2] // tk

    def body(j, carry):
        m, l, acc = carry
        start = pl.multiple_of(j * tk, tk)
        ks = k_ref[0, 0, pl.ds(start, tk), :]
        vs = v_ref[0, 0, pl.ds(start, tk), :]
        s = lax.dot_general(q, ks, (((1,), (1,)), ((), ())), preferred_element_type=F32)
        m_new = jnp.maximum(m, jnp.max(s, axis=-1, keepdims=True))
        p = jnp.exp(s - m_new)
        alpha = jnp.exp(m - m_new)
        l = alpha * l + jnp.sum(p, axis=-1, keepdims=True)
        acc = alpha * acc + jnp.dot(p.astype(BF16), vs, preferred_element_type=F32)
        return m_new, l, acc

    m0 = jnp.full((tq, 1), -jnp.inf, F32)
    l0 = jnp.zeros((tq, 1), F32)
    acc0 = jnp.zeros((tq, V_HEAD), F32)
    _, l, acc = lax.fori_loop(0, nk, body, (m0, l0, acc0))
    o_ref[0] = (acc / l).astype(o_ref.dtype)


def attention(q, k, v, tq=256, tk=512):
    b, hh, s, _ = q.shape
    return pl.pallas_call(
        functools.partial(_attn_kernel, tk=tk),
        grid=(b, hh, s // tq),
        in_specs=[
            pl.BlockSpec((1, 1, tq, QK_PAD), lambda bi, hi, qi: (bi, hi, qi, 0)),
            pl.BlockSpec((1, 1, s, QK_PAD), lambda bi, hi, qi: (bi, hi, 0, 0)),
            pl.BlockSpec((1, 1, s, V_HEAD), lambda bi, hi, qi: (bi, hi, 0, 0)),
        ],
        out_specs=pl.BlockSpec((1, tq, V_HEAD), lambda bi, hi, qi: (bi, qi, hi)),
        out_shape=jax.ShapeDtypeStruct((b, s, hh * V_HEAD), BF16),
        compiler_params=_cparams("arbitrary", "arbitrary", "arbitrary"),
        name="attention",
    )(q, k, v)


def _router_kernel(x_ref, g_ref, sh_ref, sc_ref, w_ref, b_ref, h_ref, cmb_ref):
    h = _norm_mod(x_ref[...], g_ref[...], sh_ref[0], sc_ref[0])
    h_ref[...] = h.astype(h_ref.dtype)
    logits = jnp.dot(h, w_ref[...], precision=HIGHEST, preferred_element_type=F32) + b_ref[...]
    lane = lax.broadcasted_iota(jnp.int32, logits.shape, 1)
    lane_f = lane.astype(F32)
    neg = jnp.float32(-jnp.inf)
    big = jnp.float32(4 * LANES)
    gmask = (lane >= N_EXPERTS) & (lane < N_EXPERTS + N_GROUPS)
    gl = jnp.where(gmask, logits, neg)
    ge = jnp.exp(gl - jnp.max(gl, axis=-1, keepdims=True))
    gp = ge / jnp.sum(ge, axis=-1, keepdims=True)
    g_w = jnp.max(gp, axis=-1, keepdims=True)
    g_lane = jnp.min(jnp.where(gmask & (gp == g_w), lane_f, big), axis=-1, keepdims=True)
    e_lo = (g_lane - N_EXPERTS) * EXPERTS_PER_GROUP
    emask = (lane_f >= e_lo) & (lane_f < e_lo + EXPERTS_PER_GROUP)
    el = jnp.where(emask, logits, neg)
    ee = jnp.exp(el - jnp.max(el, axis=-1, keepdims=True))
    ep = ee / jnp.sum(ee, axis=-1, keepdims=True)
    e1 = jnp.max(ep, axis=-1, keepdims=True)
    i1 = jnp.min(jnp.where(emask & (ep == e1), lane_f, big), axis=-1, keepdims=True)
    rest = emask & (lane_f != i1)
    ep2 = jnp.where(rest, ep, -1.0)
    e2 = jnp.max(ep2, axis=-1, keepdims=True)
    i2 = jnp.min(jnp.where(rest & (ep2 == e2), lane_f, big), axis=-1, keepdims=True)
    tot = e1 + e2
    cmb = jnp.where(lane_f == i1, g_w * (e1 / tot), jnp.where(lane_f == i2, g_w * (e2 / tot), 0.0))
    cmb_ref[...] = cmb


def moe_router(x, g, sh, sc, w_r, b_r, seq, tm=512):
    t, d = x.shape
    per = seq // tm
    nb = sh.shape[0]
    return pl.pallas_call(
        _router_kernel,
        grid=(t // tm,),
        in_specs=[
            pl.BlockSpec((tm, d), lambda i: (i, 0)),
            pl.BlockSpec((1, d), lambda i: (0, 0)),
            pl.BlockSpec((1, 1, d), lambda i: (i // per, 0, 0)),
            pl.BlockSpec((1, 1, d), lambda i: (i // per, 0, 0)),
            pl.BlockSpec((d, LANES), lambda i: (0, 0)),
            pl.BlockSpec((1, LANES), lambda i: (0, 0)),
        ],
        out_specs=[
            pl.BlockSpec((tm, d), lambda i: (i, 0)),
            pl.BlockSpec((tm, LANES), lambda i: (i, 0)),
        ],
        out_shape=[
            jax.ShapeDtypeStruct((t, d), BF16),
            jax.ShapeDtypeStruct((t, LANES), F32),
        ],
        compiler_params=_cparams("arbitrary"),
        name="moe_router",
    )(x, g.reshape(1, d), sh.reshape(nb, 1, d), sc.reshape(nb, 1, d), w_r, b_r)


def _moe_kernel(h_ref, cmb_ref, wg_ref, wu_ref, wd_ref, x_ref, g_ref, o_ref, acc_ref):
    e = pl.program_id(1)

    @pl.when(e == 0)
    def _():
        acc_ref[...] = jnp.zeros_like(acc_ref)

    h = h_ref[...]
    a = jnp.dot(h, wg_ref[0], preferred_element_type=F32)
    u = jnp.dot(h, wu_ref[0], preferred_element_type=F32)
    cmb = cmb_ref[...]
    lane = lax.broadcasted_iota(jnp.int32, cmb.shape, 1)
    ce = jnp.sum(jnp.where(lane == e, cmb, 0.0), axis=-1, keepdims=True)
    act = (a * jax.nn.sigmoid(a)) * u * ce
    acc_ref[...] += jnp.dot(act.astype(BF16), wd_ref[0], preferred_element_type=F32)

    @pl.when(e == pl.num_programs(1) - 1)
    def _():
        o_ref[...] = x_ref[...] + g_ref[0] * acc_ref[...]


def moe_experts(h, cmb, wg, wu, wd, x, gate, seq, tm=1024):
    t, d = h.shape
    ne, _, f = wg.shape
    per = seq // tm
    nb = gate.shape[0]
    return pl.pallas_call(
        _moe_kernel,
        grid=(t // tm, ne),
        in_specs=[
            pl.BlockSpec((tm, d), lambda i, e: (i, 0)),
            pl.BlockSpec((tm, LANES), lambda i, e: (i, 0)),
            pl.BlockSpec((1, d, f), lambda i, e: (e, 0, 0)),
            pl.BlockSpec((1, d, f), lambda i, e: (e, 0, 0)),
            pl.BlockSpec((1, f, d), lambda i, e: (e, 0, 0)),
            pl.BlockSpec((tm, d), lambda i, e: (i, 0)),
            pl.BlockSpec((1, 1, d), lambda i, e: (i // per, 0, 0)),
        ],
        out_specs=pl.BlockSpec((tm, d), lambda i, e: (i, 0)),
        out_shape=jax.ShapeDtypeStruct((t, d), F32),
        scratch_shapes=[pltpu.VMEM((tm, d), F32)],
        compiler_params=_cparams("arbitrary", "arbitrary"),
        name="moe_experts",
    )(h, cmb, wg, wu, wd, x, gate.reshape(nb, 1, d))


def _shortconv_kernel(u_ref, w_ref, b_ref, o_ref):
    u = u_ref[0].astype(F32)
    n = u.shape[0]
    row = lax.broadcasted_iota(jnp.int32, u.shape, 0)
    prev = jnp.where(row == 0, 0.0, pltpu.roll(u, 1, 0))
    nxt = jnp.where(row == n - 1, 0.0, pltpu.roll(u, n - 1, 0))
    w = w_ref[...]
    o = prev * w[0:1, :] + u * w[1:2, :] + nxt * w[2:3, :] + b_ref[...]
    o_ref[0] = o.astype(o_ref.dtype)


def short_conv(u, w, b, tc=256):
    nb, s, c = u.shape
    return pl.pallas_call(
        _shortconv_kernel,
        grid=(nb, c // tc),
        in_specs=[
            pl.BlockSpec((1, s, tc), lambda bi, ci: (bi, 0, ci)),
            pl.BlockSpec((3, tc), lambda bi, ci: (0, ci)),
            pl.BlockSpec((1, tc), lambda bi, ci: (0, ci)),
        ],
        out_specs=pl.BlockSpec((1, s, tc), lambda bi, ci: (bi, 0, ci)),
        out_shape=jax.ShapeDtypeStruct(u.shape, BF16),
        compiler_params=_cparams("arbitrary", "arbitrary"),
        name="short_conv",
    )(u, w, b.reshape(1, c))


def _filter_kernel(z_ref, w1_ref, b1_ref, f1_ref, w2_ref, b2_ref, f2_ref, w3_ref, b3_ref, f3_ref,
                   w4_ref, delta_ref, o_ref):
    z = z_ref[...]
    dot = functools.partial(jnp.dot, precision=HIGHEST, preferred_element_type=F32)
    hdn = jnp.sin(f1_ref[...] * (dot(z, w1_ref[...]) + b1_ref[...]))
    hdn = jnp.sin(f2_ref[...] * (dot(hdn, w2_ref[...]) + b2_ref[...]))
    hdn = jnp.sin(f3_ref[...] * (dot(hdn, w3_ref[...]) + b3_ref[...]))
    h = dot(hdn, w4_ref[...])
    t = z[:, 0:1]
    h = h * jnp.exp(-t * delta_ref[...])
    o_ref[...] = h / (jnp.sum(jnp.abs(h), axis=0, keepdims=True) + EPS)


def hyena_filter(z, w1, b1, f1, w2, b2, f2, w3, b3, f3, w4, deltas, tn=256):
    length, emb = z.shape
    width = w2.shape[0]
    n = w4.shape[1]
    full = lambda shape: pl.BlockSpec(shape, lambda j: (0,) * len(shape))
    vec = lambda v: v.reshape(1, -1)
    return pl.pallas_call(
        _filter_kernel,
        grid=(n // tn,),
        in_specs=[
            full((length, emb)),
            full((emb, width)), full((1, width)), full((1, width)),
            full((width, width)), full((1, width)), full((1, width)),
            full((width, width)), full((1, width)), full((1, width)),
            pl.BlockSpec((width, tn), lambda j: (0, j)),
            pl.BlockSpec((1, tn), lambda j: (0, j)),
        ],
        out_specs=pl.BlockSpec((length, tn), lambda j: (0, j)),
        out_shape=jax.ShapeDtypeStruct((length, n), F32),
        compiler_params=_cparams("arbitrary"),
        name="hyena_filter",
    )(z, w1, vec(b1), vec(f1), w2, vec(b2), vec(f2), w3, vec(b3), vec(f3), w4, vec(deltas))


def _bmm_kernel(a_ref, x_ref, o_ref):
    o_ref[0] = jnp.dot(a_ref[...], x_ref[0], preferred_element_type=F32).astype(o_ref.dtype)


def block_dft(a, x, col0, ncols, out_dtype, tn=512, name="block_dft"):
    m, k = a.shape
    g = x.shape[0]
    c0 = col0 // tn
    return pl.pallas_call(
        _bmm_kernel,
        grid=(g, ncols // tn),
        in_specs=[
            pl.BlockSpec((m, k), lambda gi, j: (0, 0)),
            pl.BlockSpec((1, k, tn), lambda gi, j: (gi, 0, c0 + j)),
        ],
        out_specs=pl.BlockSpec((1, m, tn), lambda gi, j: (gi, 0, j)),
        out_shape=jax.ShapeDtypeStruct((g, m, ncols), out_dtype),
        compiler_params=_cparams("arbitrary", "arbitrary"),
        name=name,
    )(a, x)


def _specmul_kernel(u_ref, p_ref, q_ref, r_ref, o_ref):
    nblk = u_ref.shape[1]
    for i in range(nblk):
        re_acc = None
        im_acc = None
        for j in range(nblk):
            lag = i - j + nblk - 1
            re = u_ref[0, j, 0].astype(F32)
            im = u_ref[0, j, 1].astype(F32)
            p = p_ref[lag]
            q = q_ref[lag]
            r = r_ref[lag]
            tre = re * p - im * q
            tim = re * q + im * r
            re_acc = tre if re_acc is None else re_acc + tre
            im_acc = tim if im_acc is None else im_acc + tim
        o_ref[0, i, 0] = re_acc.astype(o_ref.dtype)
        o_ref[0, i, 1] = im_acc.astype(o_ref.dtype)


def spectral_multiply(uf, p, q, r, tf=256, tn=256):
    nb, nblk, _, nf, d = uf.shape
    nlag = p.shape[0]
    hspec = pl.BlockSpec((nlag, tf, tn), lambda bi, fi, ci: (0, fi, ci))
    return pl.pallas_call(
        _specmul_kernel,
        grid=(nb, nf // tf, d // tn),
        in_specs=[pl.BlockSpec((1, nblk, 2, tf, tn), lambda bi, fi, ci: (bi, 0, 0, fi, ci)), hspec, hspec, hspec],
        out_specs=pl.BlockSpec((1, nblk, 2, tf, tn), lambda bi, fi, ci: (bi, 0, 0, fi, ci)),
        out_shape=jax.ShapeDtypeStruct(uf.shape, BF16),
        compiler_params=_cparams("arbitrary", "arbitrary", "arbitrary"),
        name="spectral_multiply",
    )(uf, p, q, r)


def _idft_gate_kernel(a_ref, yf_ref, gate_ref, u_ref, bias_ref, o_ref):
    y = jnp.dot(a_ref[...], yf_ref[0], preferred_element_type=F32)
    u = u_ref[0].astype(F32)
    o_ref[0] = (gate_ref[0].astype(F32) * (y + u * bias_ref[...])).astype(o_ref.dtype)


def idft_gate(a, yf, gate_arr, gate_col0, u_arr, u_col0, bias, tn=512, name="idft_gate"):
    m, k = a.shape
    g, _, d = yf.shape
    gc0 = gate_col0 // tn
    uc0 = u_col0 // tn
    return pl.pallas_call(
        _idft_gate_kernel,
        grid=(g, d // tn),
        in_specs=[
            pl.BlockSpec((m, k), lambda gi, j: (0, 0)),
            pl.BlockSpec((1, k, tn), lambda gi, j: (gi, 0, j)),
            pl.BlockSpec((1, m, tn), lambda gi, j: (gi, 0, gc0 + j)),
            pl.BlockSpec((1, m, tn), lambda gi, j: (gi, 0, uc0 + j)),
            pl.BlockSpec((1, tn), lambda gi, j: (0, j)),
        ],
        out_specs=pl.BlockSpec((1, m, tn), lambda gi, j: (gi, 0, j)),
        out_shape=jax.ShapeDtypeStruct((g, m, d), BF16),
        compiler_params=_cparams("arbitrary", "arbitrary"),
        name=name,
    )(a, yf, gate_arr, u_arr, bias.reshape(1, d))


def _dft_matrices(tb):
    n = 2 * tb
    k = np.arange(tb, dtype=np.float64)[:, None]
    t = np.arange(tb, dtype=np.float64)[None, :]
    ang = 2.0 * np.pi * k * t / n
    fre = np.cos(ang)
    fim = -np.sin(ang)
    fim[0, :] = np.cos(np.pi * t[0])
    fwd = np.concatenate([fre, fim], axis=0)
    wre = np.full((tb, 1), 2.0)
    wre[0, 0] = 1.0
    ire = wre * np.cos(ang) / n
    iim = -2.0 * np.sin(ang) / n
    iim[0, :] = np.cos(np.pi * t[0]) / n
    inv = np.concatenate([ire, iim], axis=0).T
    sign = np.where(np.arange(tb) % 2 == 0, 1.0, -1.0)
    return jnp.asarray(fwd, BF16), jnp.asarray(inv, BF16), jnp.asarray(sign, F32)


def _filter_spectra(filt, fwd, sign, nblk, tb):
    length = filt.shape[0]
    d = D_MODEL
    f4 = filt.reshape(length, HYENA_ORDER, 2, d)
    h_fwd = f4[:, :, 0, :]
    h_bwd = f4[:, :, 1, :]
    h2 = jnp.concatenate([jnp.zeros((1, HYENA_ORDER, d), F32), h_bwd[:0:-1], h_fwd], axis=0)
    h2 = h2.reshape(2 * nblk, tb, HYENA_ORDER * d)
    e = block_dft(fwd, h2.astype(BF16), 0, HYENA_ORDER * d, F32, name="filter_dft")
    e = e.reshape(2 * nblk, 2, tb, HYENA_ORDER * d)
    a0 = h2[:, 0, :]
    sg = sign[None, :, None]
    cur_re, cur_im = e[1:, 0], e[1:, 1]
    prv_re, prv_im = e[:-1, 0], e[:-1, 1]
    prv_a0 = a0[:-1, None, :]
    row0 = (jnp.arange(tb) == 0)[None, :, None]
    seg_re = cur_re + sg * (prv_re - prv_a0)
    seg_im = cur_im + sg * (prv_im - jnp.where(row0, prv_a0, 0.0))
    p = seg_re
    q = jnp.where(row0, 0.0, seg_im)
    r = jnp.where(row0, seg_im, seg_re)
    return p, q, r


def kernel(x, c, positions, ada_w, ada_b, norm_mix_g, norm_ffn_g, mla_w_down, mla_q_a_g, mla_kv_a_g, mla_w_uq, mla_w_ukv, mla_q_norm_g, mla_k_norm_g, mla_w_o, hy_w_in, hy_b_in, hy_conv_w, hy_conv_b, hy_f_w1, hy_f_b1, hy_f_freq1, hy_f_w2, hy_f_b2, hy_f_freq2, hy_f_w3, hy_f_b3, hy_f_freq3, hy_f_w4, hy_filt_bias, hy_w_out, moe_wg, moe_bg, moe_we, moe_be, moe_w_gate, moe_w_up, moe_w_down):
    batch, seq, d = x.shape
    t = batch * seq
    hh = MLA_HEADS
    xf = x.reshape(t, d)

    mod = adaln(c, ada_w, ada_b)

    def mods(i):
        return [mod[i, :, j * d:(j + 1) * d] for j in range(6)]

    def moe_layer(xin, i, sh2, sc2, g2):
        w_r = jnp.concatenate([moe_we[i], moe_wg[i], jnp.zeros((d, LANES - N_EXPERTS - N_GROUPS), F32)], axis=1)
        b_r = jnp.concatenate([moe_be[i], moe_bg[i], jnp.zeros((LANES - N_EXPERTS - N_GROUPS,), F32)]).reshape(1, LANES)
        h, cmb = moe_router(xin, norm_ffn_g[i], sh2, sc2, w_r, b_r, seq)
        return moe_experts(h, cmb, moe_w_gate[i].astype(BF16), moe_w_up[i].astype(BF16),
                           moe_w_down[i].astype(BF16), xin, g2, seq)

    sh1, sc1, g1, sh2, sc2, g2 = mods(0)
    lat_w = mla_w_down[0]
    w_dn = jnp.concatenate([lat_w, jnp.zeros((d, LAT_PAD - lat_w.shape[1]), F32)], axis=1).astype(BF16)
    lat = norm_mod_matmul(xf, norm_mix_g[0], sh1, sc1, w_dn, jnp.zeros((LAT_PAD,), F32), seq, F32, name="mla_down")

    inv_freq = 1.0 / (ROPE_THETA ** (jnp.arange(0, QK_ROPE, 2, dtype=F32) / QK_ROPE))
    ang = positions.astype(F32)[..., None] * inv_freq
    pad = jnp.zeros((batch, seq, LANES - QK_ROPE), F32)
    cos_t = jnp.concatenate([jnp.cos(ang), jnp.cos(ang), pad], axis=-1).reshape(t, LANES)
    sin_t = jnp.concatenate([jnp.sin(ang), jnp.sin(ang), pad], axis=-1).reshape(t, LANES)

    wq = mla_w_uq[0].reshape(Q_LORA, hh, QK_HEAD)
    wq = jnp.concatenate([wq, jnp.zeros((Q_LORA, hh, QK_PAD - QK_HEAD), F32)], axis=-1)
    wq = wq.reshape(Q_LORA, hh * QK_PAD).astype(BF16)
    wkv = mla_w_ukv[0].reshape(KV_LORA, hh, QK_NOPE + V_HEAD)
    wk = wkv[:, :, :QK_NOPE].reshape(KV_LORA, hh * QK_NOPE).astype(BF16)
    wv = wkv[:, :, QK_NOPE:].reshape(KV_LORA, hh * V_HEAD).astype(BF16)
    zpad = jnp.zeros((LANES - QK_ROPE,), F32)
    gq = jnp.concatenate([mla_q_norm_g[0], zpad]).reshape(1, QK_PAD)
    gkn = mla_k_norm_g[0][:QK_NOPE].reshape(1, LANES)
    gkp = jnp.concatenate([mla_k_norm_g[0][QK_NOPE:], zpad]).reshape(1, LANES)
    q, k, v = mla_qkv(lat, cos_t, sin_t, mla_q_a_g[0].reshape(1, Q_LORA), mla_kv_a_g[0].reshape(1, KV_LORA),
                      wq, wk, wv, gq, gkn, gkp, batch, seq)
    o = attention(q, k, v)
    xf = matmul_residual(o.reshape(t, hh * V_HEAD), mla_w_o[0].astype(BF16), xf, g1, seq, name="mla_out")
    xf = moe_layer(xf, 0, sh2, sc2, g2)

    sh1, sc1, g1, sh2, sc2, g2 = mods(1)
    nblk = CONV_BLOCKS
    tb = seq // nblk
    fwd, inv, sign = _dft_matrices(tb)

    tt = jnp.linspace(0.0, 1.0, seq, dtype=F32)[:, None]
    wfreq = 2.0 * math.pi * jnp.arange(seq, dtype=F32)[:, None] / seq
    fr = jnp.linspace(1e-4, FILTER_BANDS - 1, FILTER_BANDS, dtype=F32)[None, :]
    z = jnp.concatenate([tt, jnp.cos(fr * wfreq), -jnp.sin(fr * wfreq)], axis=-1)
    deltas = jnp.abs(jnp.linspace(math.log(FAST_DECAY) / DECAY_TARGET, math.log(SLOW_DECAY) / DECAY_TARGET, d, dtype=F32))
    filt = hyena_filter(z, hy_f_w1[0], hy_f_b1[0], hy_f_freq1[0], hy_f_w2[0], hy_f_b2[0], hy_f_freq2[0],
                        hy_f_w3[0], hy_f_b3[0], hy_f_freq3[0], hy_f_w4[0], jnp.tile(deltas, HYENA_ORDER * 2))
    p, qc, r = _filter_spectra(filt, fwd, sign, nblk, tb)

    u = norm_mod_matmul(xf, norm_mix_g[1], sh1, sc1, hy_w_in[0].astype(BF16), hy_b_in[0], seq, BF16, name="hy_in")
    uc = short_conv(u.reshape(batch, seq, 3 * d), hy_conv_w[0], hy_conv_b[0])
    ucb = uc.reshape(batch * nblk, tb, 3 * d)

    zsrc, zcol = ucb, 2 * d
    for order in range(HYENA_ORDER):
        uf = block_dft(fwd, zsrc, zcol, d, BF16, name=f"conv_dft{order}")
        sl = slice(order * d, (order + 1) * d)
        yf = spectral_multiply(uf.reshape(batch, nblk, 2, tb, d), p[:, :, sl], qc[:, :, sl], r[:, :, sl])
        zsrc = idft_gate(inv, yf.reshape(batch * nblk, 2 * tb, d), ucb, order * d, zsrc, zcol,
                         hy_filt_bias[0, order], name=f"conv_idft{order}")
        zcol = 0
    xf = matmul_residual(zsrc.reshape(t, d), hy_w_out[0].astype(BF16), xf, g1, seq, name="hy_out")
    xf = moe_layer(xf, 1, sh2, sc2, g2)
    return xf.reshape(batch, seq, d)
```

```python
import functools
import math

import jax
import jax.numpy as jnp
import numpy as np
from jax import lax
from jax.experimental import pallas as pl
from jax.experimental.pallas import tpu as pltpu

F32 = jnp.float32
BF16 = jnp.bfloat16
HIGHEST = lax.Precision.HIGHEST

D_MODEL = 1024
MLA_HEADS = 8
Q_LORA = 256
KV_LORA = 128
QK_NOPE = 128
QK_ROPE = 64
QK_HEAD = QK_NOPE + QK_ROPE
V_HEAD = 128
ROPE_HALF = QK_ROPE // 2
ROPE_THETA = 10000.0
HYENA_ORDER = 2
FILTER_EMB = 33
FILTER_BANDS = (FILTER_EMB - 1) // 2
FAST_DECAY = 0.3
SLOW_DECAY = 1.5
DECAY_TARGET = 1e-2
N_GROUPS = 4
EXPERTS_PER_GROUP = 4
N_EXPERTS = N_GROUPS * EXPERTS_PER_GROUP
D_EXPERT = 256
EPS = 1e-6

LANES = 128
QK_PAD = 2 * LANES
LAT_PAD = 512
CONV_BLOCKS = 4
VMEM_LIMIT = 56 * 1024 * 1024


def _cparams(*sem):
    return pltpu.CompilerParams(dimension_semantics=sem, vmem_limit_bytes=VMEM_LIMIT)


def _adaln_kernel(c_ref, w_ref, b_ref, o_ref):
    c = c_ref[...]
    ca = c * jax.nn.sigmoid(c)
    o_ref[0] = jnp.dot(ca, w_ref[0], precision=HIGHEST, preferred_element_type=F32) + b_ref[0]


def adaln(c, ada_w, ada_b, tn=1536):
    depth, d, n = ada_w.shape
    b = c.shape[0]
    return pl.pallas_call(
        _adaln_kernel,
        grid=(depth, n // tn),
        in_specs=[
            pl.BlockSpec((b, d), lambda i, j: (0, 0)),
            pl.BlockSpec((1, d, tn), lambda i, j: (i, 0, j)),
            pl.BlockSpec((1, 1, tn), lambda i, j: (i, 0, j)),
        ],
        out_specs=pl.BlockSpec((1, b, tn), lambda i, j: (i, 0, j)),
        out_shape=jax.ShapeDtypeStruct((depth, b, n), F32),
        compiler_params=_cparams("arbitrary", "arbitrary"),
        name="adaln",
    )(c, ada_w, ada_b.reshape(depth, 1, n))


def _norm_mod(x, g, sh, sc):
    ms = jnp.mean(x * x, axis=-1, keepdims=True)
    y = x * lax.rsqrt(ms + EPS) * g
    return y * (1.0 + sc) + sh


def _nmm_kernel(x_ref, g_ref, sh_ref, sc_ref, w_ref, b_ref, o_ref):
    h = _norm_mod(x_ref[...], g_ref[...], sh_ref[0], sc_ref[0])
    o = jnp.dot(h.astype(BF16), w_ref[...], preferred_element_type=F32) + b_ref[...]
    o_ref[...] = o.astype(o_ref.dtype)


def norm_mod_matmul(x, g, sh, sc, w, b, seq, out_dtype, tm=512, name="nmm"):
    t, d = x.shape
    n = w.shape[1]
    per = seq // tm
    nb = sh.shape[0]
    return pl.pallas_call(
        _nmm_kernel,
        grid=(t // tm,),
        in_specs=[
            pl.BlockSpec((tm, d), lambda i: (i, 0)),
            pl.BlockSpec((1, d), lambda i: (0, 0)),
            pl.BlockSpec((1, 1, d), lambda i: (i // per, 0, 0)),
            pl.BlockSpec((1, 1, d), lambda i: (i // per, 0, 0)),
            pl.BlockSpec((d, n), lambda i: (0, 0)),
            pl.BlockSpec((1, n), lambda i: (0, 0)),
        ],
        out_specs=pl.BlockSpec((tm, n), lambda i: (i, 0)),
        out_shape=jax.ShapeDtypeStruct((t, n), out_dtype),
        compiler_params=_cparams("arbitrary"),
        name=name,
    )(x, g.reshape(1, d), sh.reshape(nb, 1, d), sc.reshape(nb, 1, d), w, b.reshape(1, n))


def _mmres_kernel(a_ref, w_ref, x_ref, g_ref, o_ref):
    y = jnp.dot(a_ref[...], w_ref[...], preferred_element_type=F32)
    o_ref[...] = x_ref[...] + g_ref[0] * y


def matmul_residual(a, w, x, gate, seq, tm=512, name="mmres"):
    t, k = a.shape
    d = w.shape[1]
    per = seq // tm
    nb = gate.shape[0]
    return pl.pallas_call(
        _mmres_kernel,
        grid=(t // tm,),
        in_specs=[
            pl.BlockSpec((tm, k), lambda i: (i, 0)),
            pl.BlockSpec((k, d), lambda i: (0, 0)),
            pl.BlockSpec((tm, d), lambda i: (i, 0)),
            pl.BlockSpec((1, 1, d), lambda i: (i // per, 0, 0)),
        ],
        out_specs=pl.BlockSpec((tm, d), lambda i: (i, 0)),
        out_shape=jax.ShapeDtypeStruct((t, d), F32),
        compiler_params=_cparams("arbitrary"),
        name=name,
    )(a, w, x, gate.reshape(nb, 1, d))


def _rms(x, n):
    return x * lax.rsqrt(jnp.sum(x * x, axis=-1, keepdims=True) * (1.0 / n) + EPS)


def _qkv_kernel(lat_ref, cos_ref, sin_ref, qag_ref, kvag_ref, wq_ref, wk_ref, wv_ref,
                gq_ref, gkn_ref, gkp_ref, q_ref, k_ref, v_ref):
    lat = lat_ref[...]
    cq = lat[:, :Q_LORA]
    ckv = lat[:, Q_LORA:Q_LORA + KV_LORA]
    kpe = lat[:, Q_LORA + KV_LORA:]
    cqn = (_rms(cq, Q_LORA) * qag_ref[...]).astype(BF16)
    ckvn = (_rms(ckv, KV_LORA) * kvag_ref[...]).astype(BF16)
    qall = jnp.dot(cqn, wq_ref[...], preferred_element_type=F32)
    kall = jnp.dot(ckvn, wk_ref[...], preferred_element_type=F32)
    vall = jnp.dot(ckvn, wv_ref[...], preferred_element_type=F32)
    cos_t = cos_ref[...]
    sin_t = sin_ref[...]
    lane = lax.broadcasted_iota(jnp.int32, cos_t.shape, 1)

    def rope(pe):
        up = pltpu.roll(pe, ROPE_HALF, 1)
        down = pltpu.roll(pe, LANES - ROPE_HALF, 1)
        rot = jnp.where(lane < ROPE_HALF, -down, up)
        return pe * cos_t + rot * sin_t

    gq = gq_ref[...]
    kp = rope(_rms(kpe, QK_ROPE) * gkp_ref[...])
    scale = QK_HEAD ** -0.5 * math.log2(math.e)
    for h in range(MLA_HEADS):
        qh = qall[:, h * QK_PAD:(h + 1) * QK_PAD]
        qn = _rms(qh[:, :QK_NOPE], QK_NOPE) * gq[:, :QK_NOPE]
        qp = rope(_rms(qh[:, QK_NOPE:], QK_ROPE) * gq[:, QK_NOPE:])
        q_ref[0, h] = (jnp.concatenate([qn, qp], axis=-1) * scale).astype(q_ref.dtype)
        kn = _rms(kall[:, h * QK_NOPE:(h + 1) * QK_NOPE], QK_NOPE) * gkn_ref[...]
        k_ref[0, h] = jnp.concatenate([kn, kp], axis=-1).astype(k_ref.dtype)
        v_ref[0, h] = vall[:, h * V_HEAD:(h + 1) * V_HEAD].astype(v_ref.dtype)


def mla_qkv(lat, cos_t, sin_t, q_a_g, kv_a_g, wq, wk, wv, gq, gkn, gkp, batch, seq, tm=256):
    t = lat.shape[0]
    per = seq // tm
    hh = MLA_HEADS
    full = lambda shape: pl.BlockSpec(shape, lambda i: (0,) * len(shape))
    return pl.pallas_call(
        _qkv_kernel,
        grid=(t // tm,),
        in_specs=[
            pl.BlockSpec((tm, LAT_PAD), lambda i: (i, 0)),
            pl.BlockSpec((tm, LANES), lambda i: (i, 0)),
            pl.BlockSpec((tm, LANES), lambda i: (i, 0)),
            full((1, Q_LORA)), full((1, KV_LORA)),
            full(wq.shape), full(wk.shape), full(wv.shape),
            full((1, QK_PAD)), full((1, LANES)), full((1, LANES)),
        ],
        out_specs=[
            pl.BlockSpec((1, hh, tm, QK_PAD), lambda i: (i // per, 0, i % per, 0)),
            pl.BlockSpec((1, hh, tm, QK_PAD), lambda i: (i // per, 0, i % per, 0)),
            pl.BlockSpec((1, hh, tm, V_HEAD), lambda i: (i // per, 0, i % per, 0)),
        ],
        out_shape=[
            jax.ShapeDtypeStruct((batch, hh, seq, QK_PAD), BF16),
            jax.ShapeDtypeStruct((batch, hh, seq, QK_PAD), BF16),
            jax.ShapeDtypeStruct((batch, hh, seq, V_HEAD), BF16),
        ],
        compiler_params=_cparams("arbitrary"),
        name="mla_qkv",
    )(lat, cos_t, sin_t, q_a_g, kv_a_g, wq, wk, wv, gq, gkn, gkp)


def _attn_kernel(q_ref, k_ref, v_ref, o_ref, *, tk):
    q = q_ref[0, 0]
    tq = q.shape[0]
    nk = k_ref.shape[2] // tk

    def body(j, carry):
        m, l, acc = carry
        start = pl.multiple_of(j * tk, tk)
        ks = k_ref[0, 0, pl.ds(start, tk), :]
        vs = v_ref[0, 0, pl.ds(start, tk), :]
        s = lax.dot_general(q, ks, (((1,), (1,)), ((), ())), preferred_element_type=F32)
        m_new = jnp.maximum(m, jnp.max(s, axis=-1, keepdims=True))
        p = jnp.exp2(s - m_new)
        alpha = jnp.exp2(m - m_new)
        l = alpha * l + jnp.sum(p, axis=-1, keepdims=True)
        acc = alpha * acc + jnp.dot(p.astype(BF16), vs, preferred_element_type=F32)
        return m_new, l, acc

    m0 = jnp.full((tq, 1), -jnp.inf, F32)
    l0 = jnp.zeros((tq, 1), F32)
    acc0 = jnp.zeros((tq, V_HEAD), F32)
    _, l, acc = lax.fori_loop(0, nk, body, (m0, l0, acc0), unroll=True)
    o_ref[0] = (acc / l).astype(o_ref.dtype)


def attention(q, k, v, tq=512, tk=1024):
    tq = min(tq, q.shape[2])
    tk = min(tk, q.shape[2])
    b, hh, s, _ = q.shape
    return pl.pallas_call(
        functools.partial(_attn_kernel, tk=tk),
        grid=(b, hh, s // tq),
        in_specs=[
            pl.BlockSpec((1, 1, tq, QK_PAD), lambda bi, hi, qi: (bi, hi, qi, 0)),
            pl.BlockSpec((1, 1, s, QK_PAD), lambda bi, hi, qi: (bi, hi, 0, 0)),
            pl.BlockSpec((1, 1, s, V_HEAD), lambda bi, hi, qi: (bi, hi, 0, 0)),
        ],
        out_specs=pl.BlockSpec((1, tq, V_HEAD), lambda bi, hi, qi: (bi, qi, hi)),
        out_shape=jax.ShapeDtypeStruct((b, s, hh * V_HEAD), BF16),
        compiler_params=_cparams("arbitrary", "arbitrary", "arbitrary"),
        name="attention",
    )(q, k, v)


def _router_kernel(x_ref, g_ref, sh_ref, sc_ref, w_ref, b_ref, h_ref, cmb_ref):
    h = _norm_mod(x_ref[...], g_ref[...], sh_ref[0], sc_ref[0])
    h_ref[...] = h.astype(h_ref.dtype)
    logits = jnp.dot(h, w_ref[...], precision=HIGHEST, preferred_element_type=F32) + b_ref[...]
    lane = lax.broadcasted_iota(jnp.int32, logits.shape, 1)
    lane_f = lane.astype(F32)
    neg = jnp.float32(-jnp.inf)
    big = jnp.float32(4 * LANES)
    gmask = (lane >= N_EXPERTS) & (lane < N_EXPERTS + N_GROUPS)
    gl = jnp.where(gmask, logits, neg)
    ge = jnp.exp(gl - jnp.max(gl, axis=-1, keepdims=True))
    gp = ge / jnp.sum(ge, axis=-1, keepdims=True)
    g_w = jnp.max(gp, axis=-1, keepdims=True)
    g_lane = jnp.min(jnp.where(gmask & (gp == g_w), lane_f, big), axis=-1, keepdims=True)
    e_lo = (g_lane - N_EXPERTS) * EXPERTS_PER_GROUP
    emask = (lane_f >= e_lo) & (lane_f < e_lo + EXPERTS_PER_GROUP)
    el = jnp.where(emask, logits, neg)
    ee = jnp.exp(el - jnp.max(el, axis=-1, keepdims=True))
    ep = ee / jnp.sum(ee, axis=-1, keepdims=True)
    e1 = jnp.max(ep, axis=-1, keepdims=True)
    i1 = jnp.min(jnp.where(emask & (ep == e1), lane_f, big), axis=-1, keepdims=True)
    rest = emask & (lane_f != i1)
    ep2 = jnp.where(rest, ep, -1.0)
    e2 = jnp.max(ep2, axis=-1, keepdims=True)
    i2 = jnp.min(jnp.where(rest & (ep2 == e2), lane_f, big), axis=-1, keepdims=True)
    tot = e1 + e2
    cmb = jnp.where(lane_f == i1, g_w * (e1 / tot), jnp.where(lane_f == i2, g_w * (e2 / tot), 0.0))
    cmb_ref[...] = cmb


def moe_router(x, g, sh, sc, w_r, b_r, seq, tm=512):
    t, d = x.shape
    per = seq // tm
    nb = sh.shape[0]
    return pl.pallas_call(
        _router_kernel,
        grid=(t // tm,),
        in_specs=[
            pl.BlockSpec((tm, d), lambda i: (i, 0)),
            pl.BlockSpec((1, d), lambda i: (0, 0)),
            pl.BlockSpec((1, 1, d), lambda i: (i // per, 0, 0)),
            pl.BlockSpec((1, 1, d), lambda i: (i // per, 0, 0)),
            pl.BlockSpec((d, LANES), lambda i: (0, 0)),
            pl.BlockSpec((1, LANES), lambda i: (0, 0)),
        ],
        out_specs=[
            pl.BlockSpec((tm, d), lambda i: (i, 0)),
            pl.BlockSpec((tm, LANES), lambda i: (i, 0)),
        ],
        out_shape=[
            jax.ShapeDtypeStruct((t, d), BF16),
            jax.ShapeDtypeStruct((t, LANES), F32),
        ],
        compiler_params=_cparams("arbitrary"),
        name="moe_router",
    )(x, g.reshape(1, d), sh.reshape(nb, 1, d), sc.reshape(nb, 1, d), w_r, b_r)


def _moe_kernel(h_ref, cmb_ref, wg_ref, wu_ref, wd_ref, x_ref, g_ref, o_ref, acc_ref):
    e = pl.program_id(1)

    @pl.when(e == 0)
    def _():
        acc_ref[...] = jnp.zeros_like(acc_ref)

    h = h_ref[...]
    a = jnp.dot(h, wg_ref[0], preferred_element_type=F32)
    u = jnp.dot(h, wu_ref[0], preferred_element_type=F32)
    cmb = cmb_ref[...]
    lane = lax.broadcasted_iota(jnp.int32, cmb.shape, 1)
    ce = jnp.sum(jnp.where(lane == e, cmb, 0.0), axis=-1, keepdims=True)
    act = (a * jax.nn.sigmoid(a)) * u * ce
    acc_ref[...] += jnp.dot(act.astype(BF16), wd_ref[0], preferred_element_type=F32)

    @pl.when(e == pl.num_programs(1) - 1)
    def _():
        o_ref[...] = x_ref[...] + g_ref[0] * acc_ref[...]


def moe_experts(h, cmb, wg, wu, wd, x, gate, seq, tm=1024):
    t, d = h.shape
    ne, _, f = wg.shape
    per = seq // tm
    nb = gate.shape[0]
    return pl.pallas_call(
        _moe_kernel,
        grid=(t // tm, ne),
        in_specs=[
            pl.BlockSpec((tm, d), lambda i, e: (i, 0)),
            pl.BlockSpec((tm, LANES), lambda i, e: (i, 0)),
            pl.BlockSpec((1, d, f), lambda i, e: (e, 0, 0)),
            pl.BlockSpec((1, d, f), lambda i, e: (e, 0, 0)),
            pl.BlockSpec((1, f, d), lambda i, e: (e, 0, 0)),
            pl.BlockSpec((tm, d), lambda i, e: (i, 0)),
            pl.BlockSpec((1, 1, d), lambda i, e: (i // per, 0, 0)),
        ],
        out_specs=pl.BlockSpec((tm, d), lambda i, e: (i, 0)),
        out_shape=jax.ShapeDtypeStruct((t, d), F32),
        scratch_shapes=[pltpu.VMEM((tm, d), F32)],
        compiler_params=_cparams("arbitrary", "arbitrary"),
        name="moe_experts",
    )(h, cmb, wg, wu, wd, x, gate.reshape(nb, 1, d))


def _shortconv_kernel(u_ref, w_ref, b_ref, o_ref):
    u = u_ref[0].astype(F32)
    n = u.shape[0]
    row = lax.broadcasted_iota(jnp.int32, u.shape, 0)
    prev = jnp.where(row == 0, 0.0, pltpu.roll(u, 1, 0))
    nxt = jnp.where(row == n - 1, 0.0, pltpu.roll(u, n - 1, 0))
    w = w_ref[...]
    o = prev * w[0:1, :] + u * w[1:2, :] + nxt * w[2:3, :] + b_ref[...]
    o_ref[0] = o.astype(o_ref.dtype)


def short_conv(u, w, b, tc=256):
    nb, s, c = u.shape
    return pl.pallas_call(
        _shortconv_kernel,
        grid=(nb, c // tc),
        in_specs=[
            pl.BlockSpec((1, s, tc), lambda bi, ci: (bi, 0, ci)),
            pl.BlockSpec((3, tc), lambda bi, ci: (0, ci)),
            pl.BlockSpec((1, tc), lambda bi, ci: (0, ci)),
        ],
        out_specs=pl.BlockSpec((1, s, tc), lambda bi, ci: (bi, 0, ci)),
        out_shape=jax.ShapeDtypeStruct(u.shape, BF16),
        compiler_params=_cparams("arbitrary", "arbitrary"),
        name="short_conv",
    )(u, w, b.reshape(1, c))


def _filter_kernel(z_ref, w1_ref, b1_ref, f1_ref, w2_ref, b2_ref, f2_ref, w3_ref, b3_ref, f3_ref,
                   w4_ref, delta_ref, o_ref, hdn_ref):
    dot = functools.partial(jnp.dot, precision=HIGHEST, preferred_element_type=F32)

    @pl.when(pl.program_id(0) == 0)
    def _():
        hdn = jnp.sin(f1_ref[...] * (dot(z_ref[...], w1_ref[...]) + b1_ref[...]))
        hdn = jnp.sin(f2_ref[...] * (dot(hdn, w2_ref[...]) + b2_ref[...]))
        hdn_ref[...] = jnp.sin(f3_ref[...] * (dot(hdn, w3_ref[...]) + b3_ref[...]))

    h = dot(hdn_ref[...], w4_ref[...])
    t = z_ref[:, 0:1]
    h = h * jnp.exp(-t * delta_ref[...])
    o_ref[...] = (h / (jnp.sum(jnp.abs(h), axis=0, keepdims=True) + EPS)).astype(o_ref.dtype)


def hyena_filter(z, w1, b1, f1, w2, b2, f2, w3, b3, f3, w4, deltas, tn=256):
    length, emb = z.shape
    width = w2.shape[0]
    n = w4.shape[1]
    full = lambda shape: pl.BlockSpec(shape, lambda j: (0,) * len(shape))
    vec = lambda v: v.reshape(1, -1)
    return pl.pallas_call(
        _filter_kernel,
        grid=(n // tn,),
        in_specs=[
            full((length, emb)),
            full((emb, width)), full((1, width)), full((1, width)),
            full((width, width)), full((1, width)), full((1, width)),
            full((width, width)), full((1, width)), full((1, width)),
            pl.BlockSpec((width, tn), lambda j: (0, j)),
            pl.BlockSpec((1, tn), lambda j: (0, j)),
        ],
        out_specs=pl.BlockSpec((length, tn), lambda j: (0, j)),
        out_shape=jax.ShapeDtypeStruct((length, n), BF16),
        scratch_shapes=[pltpu.VMEM((length, width), F32)],
        compiler_params=_cparams("arbitrary"),
        name="hyena_filter",
    )(z, w1, vec(b1), vec(f1), w2, vec(b2), vec(f2), w3, vec(b3), vec(f3), w4, vec(deltas))


def _bmm_kernel(a_ref, x_ref, o_ref):
    o_ref[0] = jnp.dot(a_ref[...], x_ref[0], preferred_element_type=F32).astype(o_ref.dtype)


def block_dft(a, x, col0, ncols, out_dtype, tn=512, name="block_dft"):
    m, k = a.shape
    g = x.shape[0]
    c0 = col0 // tn
    return pl.pallas_call(
        _bmm_kernel,
        grid=(g, ncols // tn),
        in_specs=[
            pl.BlockSpec((m, k), lambda gi, j: (0, 0)),
            pl.BlockSpec((1, k, tn), lambda gi, j: (gi, 0, c0 + j)),
        ],
        out_specs=pl.BlockSpec((1, m, tn), lambda gi, j: (gi, 0, j)),
        out_shape=jax.ShapeDtypeStruct((g, m, ncols), out_dtype),
        compiler_params=_cparams("arbitrary", "arbitrary"),
        name=name,
    )(a, x)


def _specmul_kernel(u_ref, p_ref, q_ref, r_ref, o_ref):
    nblk = u_ref.shape[1]
    for i in range(nblk):
        re_acc = None
        im_acc = None
        for j in range(nblk):
            lag = i - j + nblk - 1
            re = u_ref[0, j, 0].astype(F32)
            im = u_ref[0, j, 1].astype(F32)
            p = p_ref[lag].astype(F32)
            q = q_ref[lag].astype(F32)
            r = r_ref[lag].astype(F32)
            tre = re * p - im * q
            tim = re * q + im * r
            re_acc = tre if re_acc is None else re_acc + tre
            im_acc = tim if im_acc is None else im_acc + tim
        o_ref[0, i, 0] = re_acc.astype(o_ref.dtype)
        o_ref[0, i, 1] = im_acc.astype(o_ref.dtype)


def spectral_multiply(uf, p, q, r, col0, tf=256, tn=256):
    nb, nblk, _, nf, d = uf.shape
    nlag = p.shape[0]
    c0 = col0 // tn
    hspec = pl.BlockSpec((nlag, tf, tn), lambda fi, ci, bi: (0, fi, c0 + ci))
    uspec = pl.BlockSpec((1, nblk, 2, tf, tn), lambda fi, ci, bi: (bi, 0, 0, fi, ci))
    return pl.pallas_call(
        _specmul_kernel,
        grid=(nf // tf, d // tn, nb),
        in_specs=[uspec, hspec, hspec, hspec],
        out_specs=uspec,
        out_shape=jax.ShapeDtypeStruct(uf.shape, BF16),
        compiler_params=_cparams("arbitrary", "arbitrary", "arbitrary"),
        name="spectral_multiply",
    )(uf, p, q, r)


def _idft_gate_kernel(a_ref, yf_ref, gate_ref, u_ref, bias_ref, o_ref):
    y = jnp.dot(a_ref[...], yf_ref[0], preferred_element_type=F32)
    u = u_ref[0].astype(F32)
    o_ref[0] = (gate_ref[0].astype(F32) * (y + u * bias_ref[...])).astype(o_ref.dtype)


def idft_gate(a, yf, gate_arr, gate_col0, u_arr, u_col0, bias, tn=512, name="idft_gate"):
    m, k = a.shape
    g, _, d = yf.shape
    gc0 = gate_col0 // tn
    uc0 = u_col0 // tn
    return pl.pallas_call(
        _idft_gate_kernel,
        grid=(g, d // tn),
        in_specs=[
            pl.BlockSpec((m, k), lambda gi, j: (0, 0)),
            pl.BlockSpec((1, k, tn), lambda gi, j: (gi, 0, j)),
            pl.BlockSpec((1, m, tn), lambda gi, j: (gi, 0, gc0 + j)),
            pl.BlockSpec((1, m, tn), lambda gi, j: (gi, 0, uc0 + j)),
            pl.BlockSpec((1, tn), lambda gi, j: (0, j)),
        ],
        out_specs=pl.BlockSpec((1, m, tn), lambda gi, j: (gi, 0, j)),
        out_shape=jax.ShapeDtypeStruct((g, m, d), BF16),
        compiler_params=_cparams("arbitrary", "arbitrary"),
        name=name,
    )(a, yf, gate_arr, u_arr, bias.reshape(1, d))


def _dft_matrices(tb):
    n = 2 * tb
    k = np.arange(tb, dtype=np.float64)[:, None]
    t = np.arange(tb, dtype=np.float64)[None, :]
    ang = 2.0 * np.pi * k * t / n
    fre = np.cos(ang)
    fim = -np.sin(ang)
    fim[0, :] = np.cos(np.pi * t[0])
    fwd = np.concatenate([fre, fim], axis=0)
    wre = np.full((tb, 1), 2.0)
    wre[0, 0] = 1.0
    ire = wre * np.cos(ang) / n
    iim = -2.0 * np.sin(ang) / n
    iim[0, :] = np.cos(np.pi * t[0]) / n
    inv = np.concatenate([ire, iim], axis=0).T
    sign = np.where(np.arange(tb) % 2 == 0, 1.0, -1.0)
    return jnp.asarray(fwd, BF16), jnp.asarray(inv, BF16), jnp.asarray(sign, F32)


def _fspec_kernel(ef_ref, eb_ref, tf_ref, tb_ref, s_ref, p_ref, q_ref, r_ref):
    nblk = ef_ref.shape[0]
    sgn = s_ref[...]
    shape = ef_ref.shape[2:]
    row0 = (lax.broadcasted_iota(jnp.int32, shape, 0) == 0) & (pl.program_id(1) == 0)

    def block(m):
        if m >= 0:
            return ef_ref[m, 0], ef_ref[m, 1], tf_ref[m]
        mu = -m
        tap = tb_ref[mu - 1]
        xre = eb_ref[mu - 1, 0] - tap
        xim = eb_ref[mu - 1, 1] - jnp.where(row0, tap, 0.0)
        b0 = tb_ref[mu] if mu < nblk else jnp.zeros_like(tap)
        return b0 + sgn * xre, jnp.where(row0, b0 + xim, -sgn * xim), b0

    prev = block(-nblk)
    for m in range(-(nblk - 1), nblk):
        cur = block(m)
        hre = cur[0] + sgn * (prev[0] - prev[2])
        him = cur[1] + sgn * (prev[1] - jnp.where(row0, prev[2], 0.0))
        lag = m + nblk - 1
        p_ref[lag] = hre.astype(p_ref.dtype)
        q_ref[lag] = jnp.where(row0, 0.0, him).astype(q_ref.dtype)
        r_ref[lag] = jnp.where(row0, him, hre).astype(r_ref.dtype)
        prev = cur


def filter_spectra(filt, fwd, sign, nblk, tb, tf=256, tn=256):
    d = D_MODEL
    ncol = filt.shape[1]
    fb = filt.reshape(nblk, tb, ncol)
    e = block_dft(fwd, fb, 0, ncol, F32, name="filter_dft").reshape(nblk, 2, tb, ncol)
    taps = fb[:, 0:1, :].astype(F32)
    nlag = 2 * nblk - 1
    per = d // tn
    espec = lambda direction: pl.BlockSpec(
        (nblk, 2, tf, tn), lambda o, fi, ci: (0, 0, fi, (2 * o + direction) * per + ci))
    tspec = lambda direction: pl.BlockSpec(
        (nblk, 1, tn), lambda o, fi, ci: (0, 0, (2 * o + direction) * per + ci))
    ospec = pl.BlockSpec((nlag, tf, tn), lambda o, fi, ci: (0, fi, o * per + ci))
    oshape = jax.ShapeDtypeStruct((nlag, tb, HYENA_ORDER * d), BF16)
    return pl.pallas_call(
        _fspec_kernel,
        grid=(HYENA_ORDER, tb // tf, per),
        in_specs=[espec(0), espec(1), tspec(0), tspec(1), pl.BlockSpec((tf, 1), lambda o, fi, ci: (fi, 0))],
        out_specs=[ospec, ospec, ospec],
        out_shape=[oshape, oshape, oshape],
        compiler_params=_cparams("arbitrary", "arbitrary", "arbitrary"),
        name="filter_spectra",
    )(e, e, taps, taps, sign.reshape(tb, 1))


def kernel(x, c, positions, ada_w, ada_b, norm_mix_g, norm_ffn_g, mla_w_down, mla_q_a_g, mla_kv_a_g, mla_w_uq, mla_w_ukv, mla_q_norm_g, mla_k_norm_g, mla_w_o, hy_w_in, hy_b_in, hy_conv_w, hy_conv_b, hy_f_w1, hy_f_b1, hy_f_freq1, hy_f_w2, hy_f_b2, hy_f_freq2, hy_f_w3, hy_f_b3, hy_f_freq3, hy_f_w4, hy_filt_bias, hy_w_out, moe_wg, moe_bg, moe_we, moe_be, moe_w_gate, moe_w_up, moe_w_down):
    batch, seq, d = x.shape
    t = batch * seq
    hh = MLA_HEADS
    xf = x.reshape(t, d)

    mod = adaln(c, ada_w, ada_b)

    def mods(i):
        return [mod[i, :, j * d:(j + 1) * d] for j in range(6)]

    def moe_layer(xin, i, sh2, sc2, g2):
        w_r = jnp.concatenate([moe_we[i], moe_wg[i], jnp.zeros((d, LANES - N_EXPERTS - N_GROUPS), F32)], axis=1)
        b_r = jnp.concatenate([moe_be[i], moe_bg[i], jnp.zeros((LANES - N_EXPERTS - N_GROUPS,), F32)]).reshape(1, LANES)
        h, cmb = moe_router(xin, norm_ffn_g[i], sh2, sc2, w_r, b_r, seq)
        return moe_experts(h, cmb, moe_w_gate[i].astype(BF16), moe_w_up[i].astype(BF16),
                           moe_w_down[i].astype(BF16), xin, g2, seq)

    sh1, sc1, g1, sh2, sc2, g2 = mods(0)
    lat_w = mla_w_down[0]
    w_dn = jnp.concatenate([lat_w, jnp.zeros((d, LAT_PAD - lat_w.shape[1]), F32)], axis=1).astype(BF16)
    lat = norm_mod_matmul(xf, norm_mix_g[0], sh1, sc1, w_dn, jnp.zeros((LAT_PAD,), F32), seq, F32, name="mla_down")

    inv_freq = 1.0 / (ROPE_THETA ** (jnp.arange(0, QK_ROPE, 2, dtype=F32) / QK_ROPE))
    ang = positions.astype(F32)[..., None] * inv_freq
    pad = jnp.zeros((batch, seq, LANES - QK_ROPE), F32)
    cos_t = jnp.concatenate([jnp.cos(ang), jnp.cos(ang), pad], axis=-1).reshape(t, LANES)
    sin_t = jnp.concatenate([jnp.sin(ang), jnp.sin(ang), pad], axis=-1).reshape(t, LANES)

    wq = mla_w_uq[0].reshape(Q_LORA, hh, QK_HEAD)
    wq = jnp.concatenate([wq, jnp.zeros((Q_LORA, hh, QK_PAD - QK_HEAD), F32)], axis=-1)
    wq = wq.reshape(Q_LORA, hh * QK_PAD).astype(BF16)
    wkv = mla_w_ukv[0].reshape(KV_LORA, hh, QK_NOPE + V_HEAD)
    wk = wkv[:, :, :QK_NOPE].reshape(KV_LORA, hh * QK_NOPE).astype(BF16)
    wv = wkv[:, :, QK_NOPE:].reshape(KV_LORA, hh * V_HEAD).astype(BF16)
    zpad = jnp.zeros((LANES - QK_ROPE,), F32)
    gq = jnp.concatenate([mla_q_norm_g[0], zpad]).reshape(1, QK_PAD)
    gkn = mla_k_norm_g[0][:QK_NOPE].reshape(1, LANES)
    gkp = jnp.concatenate([mla_k_norm_g[0][QK_NOPE:], zpad]).reshape(1, LANES)
    q, k, v = mla_qkv(lat, cos_t, sin_t, mla_q_a_g[0].reshape(1, Q_LORA), mla_kv_a_g[0].reshape(1, KV_LORA),
                      wq, wk, wv, gq, gkn, gkp, batch, seq)
    o = attention(q, k, v)
    xf = matmul_residual(o.reshape(t, hh * V_HEAD), mla_w_o[0].astype(BF16), xf, g1, seq, name="mla_out")
    xf = moe_layer(xf, 0, sh2, sc2, g2)

    sh1, sc1, g1, sh2, sc2, g2 = mods(1)
    nblk = CONV_BLOCKS
    tb = seq // nblk
    fwd, inv, sign = _dft_matrices(tb)

    tt = jnp.linspace(0.0, 1.0, seq, dtype=F32)[:, None]
    wfreq = 2.0 * math.pi * jnp.arange(seq, dtype=F32)[:, None] / seq
    fr = jnp.linspace(1e-4, FILTER_BANDS - 1, FILTER_BANDS, dtype=F32)[None, :]
    z = jnp.concatenate([tt, jnp.cos(fr * wfreq), -jnp.sin(fr * wfreq)], axis=-1)
    deltas = jnp.abs(jnp.linspace(math.log(FAST_DECAY) / DECAY_TARGET, math.log(SLOW_DECAY) / DECAY_TARGET, d, dtype=F32))
    filt = hyena_filter(z, hy_f_w1[0], hy_f_b1[0], hy_f_freq1[0], hy_f_w2[0], hy_f_b2[0], hy_f_freq2[0],
                        hy_f_w3[0], hy_f_b3[0], hy_f_freq3[0], hy_f_w4[0], jnp.tile(deltas, HYENA_ORDER * 2))
    p, qc, r = filter_spectra(filt, fwd, sign, nblk, tb)

    u = norm_mod_matmul(xf, norm_mix_g[1], sh1, sc1, hy_w_in[0].astype(BF16), hy_b_in[0], seq, BF16, name="hy_in")
    uc = short_conv(u.reshape(batch, seq, 3 * d), hy_conv_w[0], hy_conv_b[0])
    ucb = uc.reshape(batch * nblk, tb, 3 * d)

    zsrc, zcol = ucb, 2 * d
    for order in range(HYENA_ORDER):
        uf = block_dft(fwd, zsrc, zcol, d, BF16, name=f"conv_dft{order}")
        yf = spectral_multiply(uf.reshape(batch, nblk, 2, tb, d), p, qc, r, order * d)
        zsrc = idft_gate(inv, yf.reshape(batch * nblk, 2 * tb, d), ucb, order * d, zsrc, zcol,
                         hy_filt_bias[0, order], name=f"conv_idft{order}")
        zcol = 0
    xf = matmul_residual(zsrc.reshape(t, d), hy_w_out[0].astype(BF16), xf, g1, seq, name="hy_out")
    xf = moe_layer(xf, 1, sh2, sc2, g2)
    return xf.reshape(batch, seq, d)
```

```python
import functools
import math

import jax
import jax.numpy as jnp
import numpy as np
from jax import lax
from jax.experimental import pallas as pl
from jax.experimental.pallas import tpu as pltpu

F32 = jnp.float32
BF16 = jnp.bfloat16
HIGHEST = lax.Precision.HIGHEST

D_MODEL = 1024
MLA_HEADS = 8
Q_LORA = 256
KV_LORA = 128
QK_NOPE = 128
QK_ROPE = 64
QK_HEAD = QK_NOPE + QK_ROPE
V_HEAD = 128
ROPE_HALF = QK_ROPE // 2
ROPE_THETA = 10000.0
HYENA_ORDER = 2
FILTER_EMB = 33
FILTER_BANDS = (FILTER_EMB - 1) // 2
FAST_DECAY = 0.3
SLOW_DECAY = 1.5
DECAY_TARGET = 1e-2
N_GROUPS = 4
EXPERTS_PER_GROUP = 4
N_EXPERTS = N_GROUPS * EXPERTS_PER_GROUP
D_EXPERT = 256
EPS = 1e-6

LANES = 128
QK_PAD = 2 * LANES
LAT_PAD = 512
CONV_BLOCKS = 4
MOE_TILE = 1024
MOE_CHUNK = 128
MOE_SLAB = 256
VMEM_LIMIT = 56 * 1024 * 1024


def _cparams(*sem):
    return pltpu.CompilerParams(dimension_semantics=sem, vmem_limit_bytes=VMEM_LIMIT)


def _adaln_kernel(c_ref, w_ref, b_ref, o_ref):
    c = c_ref[...]
    ca = c * jax.nn.sigmoid(c)
    o_ref[0] = jnp.dot(ca, w_ref[0], precision=HIGHEST, preferred_element_type=F32) + b_ref[0]


def adaln(c, ada_w, ada_b, tn=1536):
    depth, d, n = ada_w.shape
    b = c.shape[0]
    return pl.pallas_call(
        _adaln_kernel,
        grid=(depth, n // tn),
        in_specs=[
            pl.BlockSpec((b, d), lambda i, j: (0, 0)),
            pl.BlockSpec((1, d, tn), lambda i, j: (i, 0, j)),
            pl.BlockSpec((1, 1, tn), lambda i, j: (i, 0, j)),
        ],
        out_specs=pl.BlockSpec((1, b, tn), lambda i, j: (i, 0, j)),
        out_shape=jax.ShapeDtypeStruct((depth, b, n), F32),
        compiler_params=_cparams("arbitrary", "arbitrary"),
        name="adaln",
    )(c, ada_w, ada_b.reshape(depth, 1, n))


def _norm_mod(x, g, sh, sc):
    ms = jnp.mean(x * x, axis=-1, keepdims=True)
    y = x * lax.rsqrt(ms + EPS) * g
    return y * (1.0 + sc) + sh


def _nmm_kernel(x_ref, g_ref, sh_ref, sc_ref, w_ref, b_ref, o_ref):
    h = _norm_mod(x_ref[...], g_ref[...], sh_ref[0], sc_ref[0])
    o = jnp.dot(h.astype(BF16), w_ref[...], preferred_element_type=F32) + b_ref[...]
    o_ref[...] = o.astype(o_ref.dtype)


def norm_mod_matmul(x, g, sh, sc, w, b, seq, out_dtype, tm=512, name="nmm"):
    t, d = x.shape
    n = w.shape[1]
    per = seq // tm
    nb = sh.shape[0]
    return pl.pallas_call(
        _nmm_kernel,
        grid=(t // tm,),
        in_specs=[
            pl.BlockSpec((tm, d), lambda i: (i, 0)),
            pl.BlockSpec((1, d), lambda i: (0, 0)),
            pl.BlockSpec((1, 1, d), lambda i: (i // per, 0, 0)),
            pl.BlockSpec((1, 1, d), lambda i: (i // per, 0, 0)),
            pl.BlockSpec((d, n), lambda i: (0, 0)),
            pl.BlockSpec((1, n), lambda i: (0, 0)),
        ],
        out_specs=pl.BlockSpec((tm, n), lambda i: (i, 0)),
        out_shape=jax.ShapeDtypeStruct((t, n), out_dtype),
        compiler_params=_cparams("arbitrary"),
        name=name,
    )(x, g.reshape(1, d), sh.reshape(nb, 1, d), sc.reshape(nb, 1, d), w, b.reshape(1, n))


def _mmres_kernel(a_ref, w_ref, x_ref, g_ref, o_ref):
    y = jnp.dot(a_ref[...], w_ref[...], preferred_element_type=F32)
    o_ref[...] = x_ref[...] + g_ref[0] * y


def matmul_residual(a, w, x, gate, seq, tm=512, name="mmres"):
    t, k = a.shape
    d = w.shape[1]
    per = seq // tm
    nb = gate.shape[0]
    return pl.pallas_call(
        _mmres_kernel,
        grid=(t // tm,),
        in_specs=[
            pl.BlockSpec((tm, k), lambda i: (i, 0)),
            pl.BlockSpec((k, d), lambda i: (0, 0)),
            pl.BlockSpec((tm, d), lambda i: (i, 0)),
            pl.BlockSpec((1, 1, d), lambda i: (i // per, 0, 0)),
        ],
        out_specs=pl.BlockSpec((tm, d), lambda i: (i, 0)),
        out_shape=jax.ShapeDtypeStruct((t, d), F32),
        compiler_params=_cparams("arbitrary"),
        name=name,
    )(a, w, x, gate.reshape(nb, 1, d))


def _rms(x, n):
    return x * lax.rsqrt(jnp.sum(x * x, axis=-1, keepdims=True) * (1.0 / n) + EPS)


def _qkv_kernel(lat_ref, cos_ref, sin_ref, qag_ref, kvag_ref, wq_ref, wk_ref, wv_ref,
                gq_ref, gkn_ref, gkp_ref, q_ref, k_ref, v_ref):
    lat = lat_ref[...]
    cq = lat[:, :Q_LORA]
    ckv = lat[:, Q_LORA:Q_LORA + KV_LORA]
    kpe = lat[:, Q_LORA + KV_LORA:]
    cqn = (_rms(cq, Q_LORA) * qag_ref[...]).astype(BF16)
    ckvn = (_rms(ckv, KV_LORA) * kvag_ref[...]).astype(BF16)
    qall = jnp.dot(cqn, wq_ref[...], preferred_element_type=F32)
    kall = jnp.dot(ckvn, wk_ref[...], preferred_element_type=F32)
    vall = jnp.dot(ckvn, wv_ref[...], preferred_element_type=F32)
    cos_t = cos_ref[...]
    sin_t = sin_ref[...]
    lane = lax.broadcasted_iota(jnp.int32, cos_t.shape, 1)

    def rope(pe):
        up = pltpu.roll(pe, ROPE_HALF, 1)
        down = pltpu.roll(pe, LANES - ROPE_HALF, 1)
        rot = jnp.where(lane < ROPE_HALF, -down, up)
        return pe * cos_t + rot * sin_t

    gq = gq_ref[...]
    kp = rope(_rms(kpe, QK_ROPE) * gkp_ref[...])
    scale = QK_HEAD ** -0.5 * math.log2(math.e)
    for h in range(MLA_HEADS):
        qh = qall[:, h * QK_PAD:(h + 1) * QK_PAD]
        qn = _rms(qh[:, :QK_NOPE], QK_NOPE) * gq[:, :QK_NOPE]
        qp = rope(_rms(qh[:, QK_NOPE:], QK_ROPE) * gq[:, QK_NOPE:])
        q_ref[0, h] = (jnp.concatenate([qn, qp], axis=-1) * scale).astype(q_ref.dtype)
        kn = _rms(kall[:, h * QK_NOPE:(h + 1) * QK_NOPE], QK_NOPE) * gkn_ref[...]
        k_ref[0, h] = jnp.concatenate([kn, kp], axis=-1).astype(k_ref.dtype)
        v_ref[0, h] = vall[:, h * V_HEAD:(h + 1) * V_HEAD].astype(v_ref.dtype)


def mla_qkv(lat, cos_t, sin_t, q_a_g, kv_a_g, wq, wk, wv, gq, gkn, gkp, batch, seq, tm=256):
    t = lat.shape[0]
    per = seq // tm
    hh = MLA_HEADS
    full = lambda shape: pl.BlockSpec(shape, lambda i: (0,) * len(shape))
    return pl.pallas_call(
        _qkv_kernel,
        grid=(t // tm,),
        in_specs=[
            pl.BlockSpec((tm, LAT_PAD), lambda i: (i, 0)),
            pl.BlockSpec((tm, LANES), lambda i: (i, 0)),
            pl.BlockSpec((tm, LANES), lambda i: (i, 0)),
            full((1, Q_LORA)), full((1, KV_LORA)),
            full(wq.shape), full(wk.shape), full(wv.shape),
            full((1, QK_PAD)), full((1, LANES)), full((1, LANES)),
        ],
        out_specs=[
            pl.BlockSpec((1, hh, tm, QK_PAD), lambda i: (i // per, 0, i % per, 0)),
            pl.BlockSpec((1, hh, tm, QK_PAD), lambda i: (i // per, 0, i % per, 0)),
            pl.BlockSpec((1, hh, tm, V_HEAD), lambda i: (i // per, 0, i % per, 0)),
        ],
        out_shape=[
            jax.ShapeDtypeStruct((batch, hh, seq, QK_PAD), BF16),
            jax.ShapeDtypeStruct((batch, hh, seq, QK_PAD), BF16),
            jax.ShapeDtypeStruct((batch, hh, seq, V_HEAD), BF16),
        ],
        compiler_params=_cparams("arbitrary"),
        name="mla_qkv",
    )(lat, cos_t, sin_t, q_a_g, kv_a_g, wq, wk, wv, gq, gkn, gkp)


def _attn_kernel(q_ref, k_ref, v_ref, o_ref, *, tk):
    q = q_ref[0, 0]
    tq = q.shape[0]
    nk = k_ref.shape[2] // tk

    def body(j, carry):
        m, l, acc = carry
        start = pl.multiple_of(j * tk, tk)
        ks = k_ref[0, 0, pl.ds(start, tk), :]
        vs = v_ref[0, 0, pl.ds(start, tk), :]
        s = lax.dot_general(q, ks, (((1,), (1,)), ((), ())), preferred_element_type=F32)
        m_new = jnp.maximum(m, jnp.max(s, axis=-1, keepdims=True))
        p = jnp.exp2(s - m_new)
        alpha = jnp.exp2(m - m_new)
        l = alpha * l + jnp.sum(p, axis=-1, keepdims=True)
        acc = alpha * acc + jnp.dot(p.astype(BF16), vs, preferred_element_type=F32)
        return m_new, l, acc

    m0 = jnp.full((tq, 1), -jnp.inf, F32)
    l0 = jnp.zeros((tq, 1), F32)
    acc0 = jnp.zeros((tq, V_HEAD), F32)
    _, l, acc = lax.fori_loop(0, nk, body, (m0, l0, acc0), unroll=True)
    o_ref[0] = (acc / l).astype(o_ref.dtype)


def attention(q, k, v, tq=512, tk=1024):
    tq = min(tq, q.shape[2])
    tk = min(tk, q.shape[2])
    b, hh, s, _ = q.shape
    return pl.pallas_call(
        functools.partial(_attn_kernel, tk=tk),
        grid=(b, hh, s // tq),
        in_specs=[
            pl.BlockSpec((1, 1, tq, QK_PAD), lambda bi, hi, qi: (bi, hi, qi, 0)),
            pl.BlockSpec((1, 1, s, QK_PAD), lambda bi, hi, qi: (bi, hi, 0, 0)),
            pl.BlockSpec((1, 1, s, V_HEAD), lambda bi, hi, qi: (bi, hi, 0, 0)),
        ],
        out_specs=pl.BlockSpec((1, tq, V_HEAD), lambda bi, hi, qi: (bi, qi, hi)),
        out_shape=jax.ShapeDtypeStruct((b, s, hh * V_HEAD), BF16),
        compiler_params=_cparams("arbitrary", "arbitrary", "arbitrary"),
        name="attention",
    )(q, k, v)


def _router_kernel(x_ref, g_ref, sh_ref, sc_ref, w_ref, b_ref, tri_ref, h_ref, meta_ref, cnt_ref):
    tm, d = x_ref.shape
    h = _norm_mod(x_ref[...], g_ref[...], sh_ref[0], sc_ref[0])
    logits = jnp.dot(h, w_ref[...], precision=HIGHEST, preferred_element_type=F32) + b_ref[...]
    lane = lax.broadcasted_iota(jnp.int32, logits.shape, 1)
    lane_f = lane.astype(F32)
    neg = jnp.float32(-jnp.inf)
    big = jnp.float32(4 * LANES)
    gmask = (lane >= N_EXPERTS) & (lane < N_EXPERTS + N_GROUPS)
    gl = jnp.where(gmask, logits, neg)
    ge = jnp.exp(gl - jnp.max(gl, axis=-1, keepdims=True))
    gp = ge / jnp.sum(ge, axis=-1, keepdims=True)
    g_w = jnp.max(gp, axis=-1, keepdims=True)
    g_lane = jnp.min(jnp.where(gmask & (gp == g_w), lane_f, big), axis=-1, keepdims=True)
    grp = g_lane - N_EXPERTS
    e_lo = grp * EXPERTS_PER_GROUP
    emask = (lane_f >= e_lo) & (lane_f < e_lo + EXPERTS_PER_GROUP)
    el = jnp.where(emask, logits, neg)
    ee = jnp.exp(el - jnp.max(el, axis=-1, keepdims=True))
    ep = ee / jnp.sum(ee, axis=-1, keepdims=True)
    e1 = jnp.max(ep, axis=-1, keepdims=True)
    i1 = jnp.min(jnp.where(emask & (ep == e1), lane_f, big), axis=-1, keepdims=True)
    rest = emask & (lane_f != i1)
    ep2 = jnp.where(rest, ep, -1.0)
    e2 = jnp.max(ep2, axis=-1, keepdims=True)
    i2 = jnp.min(jnp.where(rest & (ep2 == e2), lane_f, big), axis=-1, keepdims=True)
    tot = e1 + e2
    w4 = jnp.where(lane_f == i1 - e_lo, g_w * (e1 / tot), jnp.where(lane_f == i2 - e_lo, g_w * (e2 / tot), 0.0))
    hi = w4.astype(BF16).astype(F32)
    mid = (w4 - hi).astype(BF16).astype(F32)
    lo = (w4 - hi - mid).astype(BF16).astype(F32)
    ext = hi + pltpu.roll(mid, EXPERTS_PER_GROUP, 1) + pltpu.roll(lo, 2 * EXPERTS_PER_GROUP, 1)
    h_ref[:, :d] = h.astype(h_ref.dtype)
    h_ref[:, d:] = ext.astype(h_ref.dtype)
    onehot = lane_f == grp
    ohf = jnp.where(onehot, 1.0, 0.0)
    counts = jnp.broadcast_to(jnp.sum(ohf, axis=0, keepdims=True), (8, LANES))
    starts = pltpu.roll(counts, 1, 1) + pltpu.roll(counts, 2, 1) + pltpu.roll(counts, 3, 1)
    rank = jnp.dot(tri_ref[...], ohf.astype(BF16), preferred_element_type=F32)
    pos = jnp.sum(jnp.where(onehot, rank - 1.0 + starts[0:1], 0.0), axis=-1, keepdims=True)
    meta_ref[...] = jnp.broadcast_to(pos, meta_ref.shape)
    cnt_ref[0] = counts


def moe_router(x, g, sh, sc, w_r, b_r, seq, tm):
    t, d = x.shape
    per = seq // tm
    nb = sh.shape[0]
    tri = jnp.tril(jnp.ones((tm, tm), BF16))
    return pl.pallas_call(
        _router_kernel,
        grid=(t // tm,),
        in_specs=[
            pl.BlockSpec((tm, d), lambda i: (i, 0)),
            pl.BlockSpec((1, d), lambda i: (0, 0)),
            pl.BlockSpec((1, 1, d), lambda i: (i // per, 0, 0)),
            pl.BlockSpec((1, 1, d), lambda i: (i // per, 0, 0)),
            pl.BlockSpec((d, LANES), lambda i: (0, 0)),
            pl.BlockSpec((1, LANES), lambda i: (0, 0)),
            pl.BlockSpec((tm, tm), lambda i: (0, 0)),
        ],
        out_specs=[
            pl.BlockSpec((tm, d + LANES), lambda i: (i, 0)),
            pl.BlockSpec((tm, LANES), lambda i: (i, 0)),
            pl.BlockSpec((1, 8, LANES), lambda i: (i, 0, 0)),
        ],
        out_shape=[
            jax.ShapeDtypeStruct((t, d + LANES), BF16),
            jax.ShapeDtypeStruct((t, LANES), F32),
            jax.ShapeDtypeStruct((t // tm, 8, LANES), F32),
        ],
        compiler_params=_cparams("arbitrary"),
        name="moe_router",
    )(x, g.reshape(1, d), sh.reshape(nb, 1, d), sc.reshape(nb, 1, d), w_r, b_r, tri)


def _moe_kernel(st_ref, en_ref, h_ref, meta_ref, posr_ref, wg_ref, wu_ref, wd_ref, x_ref, gate_ref, o_ref,
                hs_ref, y_ref):
    i = pl.program_id(0)
    g = pl.program_id(1)
    tm, d = x_ref.shape
    slab = min(MOE_SLAB, tm)

    @pl.when(g == 0)
    def _():
        pos_row = posr_ref[0]
        for r in range(0, tm, slab):
            rows = lax.broadcasted_iota(jnp.int32, (slab, tm), 0).astype(F32) + float(r)
            perm = jnp.where(rows == pos_row, 1.0, 0.0).astype(BF16)
            hs_ref[r:r + slab] = jnp.dot(perm, h_ref[...], preferred_element_type=F32).astype(BF16)
        y_ref[...] = jnp.zeros_like(y_ref)

    start = st_ref[i * N_GROUPS + g]
    end = en_ref[i * N_GROUPS + g]
    shift = MOE_CHUNK.bit_length() - 1
    c_lo = lax.shift_right_logical(start, shift)
    c_hi = jnp.where(end > start, lax.shift_right_logical(end + (MOE_CHUNK - 1), shift), c_lo)

    def chunk(c, carry):
        r0 = pl.multiple_of(c * MOE_CHUNK, MOE_CHUNK)
        hx = hs_ref[pl.ds(r0, MOE_CHUNK), :]
        hs = hx[:, :d]
        ext = hx[:, d:].astype(F32)
        w4 = (ext + pltpu.roll(ext, LANES - EXPERTS_PER_GROUP, 1)
              + pltpu.roll(ext, LANES - 2 * EXPERTS_PER_GROUP, 1))
        rowid = r0 + lax.broadcasted_iota(jnp.int32, w4.shape, 0)
        w4 = jnp.where((rowid >= start) & (rowid < end), w4, 0.0)
        lane = lax.broadcasted_iota(jnp.int32, w4.shape, 1)
        acc = jnp.zeros((MOE_CHUNK, d), F32)
        for j in range(EXPERTS_PER_GROUP):
            a = jnp.dot(hs, wg_ref[j], preferred_element_type=F32)
            u = jnp.dot(hs, wu_ref[j], preferred_element_type=F32)
            wj = jnp.sum(jnp.where(lane == j, w4, 0.0), axis=-1, keepdims=True)
            act = (a * jax.nn.sigmoid(a)) * u * wj
            acc = acc + jnp.dot(act.astype(BF16), wd_ref[j], preferred_element_type=F32)
        y_ref[pl.ds(r0, MOE_CHUNK), :] += acc
        return carry

    lax.fori_loop(c_lo, c_hi, chunk, 0)

    @pl.when(g == N_GROUPS - 1)
    def _():
        yb = y_ref[...].astype(BF16)
        for r in range(0, tm, slab):
            pos_col = meta_ref[r:r + slab, 0:1]
            cols = lax.broadcasted_iota(jnp.int32, (slab, tm), 1).astype(F32)
            perm_t = jnp.where(cols == pos_col, 1.0, 0.0).astype(BF16)
            y = jnp.dot(perm_t, yb, preferred_element_type=F32)
            o_ref[r:r + slab] = x_ref[r:r + slab] + gate_ref[0] * y


def moe_experts(h_ext, meta, counts, wg, wu, wd, x, gate, seq, tm):
    t, d = x.shape
    f = wg.shape[2]
    per = seq // tm
    nb = gate.shape[0]
    ntiles = t // tm
    cnt = counts[:, 0, :N_GROUPS].astype(jnp.int32)
    ends = jnp.cumsum(cnt, axis=1)
    starts = ends - cnt
    pos_row = meta[:, 0].reshape(ntiles, 1, tm)
    epg = EXPERTS_PER_GROUP
    grid_spec = pltpu.PrefetchScalarGridSpec(
        num_scalar_prefetch=2,
        grid=(ntiles, N_GROUPS),
        in_specs=[
            pl.BlockSpec((tm, d + LANES), lambda i, g, st, en: (i, 0)),
            pl.BlockSpec((tm, LANES), lambda i, g, st, en: (i, 0)),
            pl.BlockSpec((1, 1, tm), lambda i, g, st, en: (i, 0, 0)),
            pl.BlockSpec((epg, d, f), lambda i, g, st, en: (g, 0, 0)),
            pl.BlockSpec((epg, d, f), lambda i, g, st, en: (g, 0, 0)),
            pl.BlockSpec((epg, f, d), lambda i, g, st, en: (g, 0, 0)),
            pl.BlockSpec((tm, d), lambda i, g, st, en: (i, 0)),
            pl.BlockSpec((1, 1, d), lambda i, g, st, en: (i // per, 0, 0)),
        ],
        out_specs=pl.BlockSpec((tm, d), lambda i, g, st, en: (i, 0)),
        scratch_shapes=[pltpu.VMEM((tm, d + LANES), BF16), pltpu.VMEM((tm, d), F32)],
    )
    return pl.pallas_call(
        _moe_kernel,
        grid_spec=grid_spec,
        out_shape=jax.ShapeDtypeStruct((t, d), F32),
        compiler_params=_cparams("arbitrary", "arbitrary"),
        name="moe_experts",
    )(starts.reshape(-1), ends.reshape(-1), h_ext, meta, pos_row, wg, wu, wd, x, gate.reshape(nb, 1, d))


def _filter_kernel(z_ref, w1_ref, b1_ref, f1_ref, w2_ref, b2_ref, f2_ref, w3_ref, b3_ref, f3_ref,
                   w4_ref, delta_ref, o_ref, hdn_ref):
    dot = functools.partial(jnp.dot, precision=HIGHEST, preferred_element_type=F32)

    @pl.when(pl.program_id(0) == 0)
    def _():
        hdn = jnp.sin(f1_ref[...] * (dot(z_ref[...], w1_ref[...]) + b1_ref[...]))
        hdn = jnp.sin(f2_ref[...] * (dot(hdn, w2_ref[...]) + b2_ref[...]))
        hdn_ref[...] = jnp.sin(f3_ref[...] * (dot(hdn, w3_ref[...]) + b3_ref[...]))

    h = dot(hdn_ref[...], w4_ref[...])
    t = z_ref[:, 0:1]
    h = h * jnp.exp(-t * delta_ref[...])
    o_ref[...] = (h / (jnp.sum(jnp.abs(h), axis=0, keepdims=True) + EPS)).astype(o_ref.dtype)


def hyena_filter(z, w1, b1, f1, w2, b2, f2, w3, b3, f3, w4, deltas, tn=256):
    length, emb = z.shape
    width = w2.shape[0]
    n = w4.shape[1]
    full = lambda shape: pl.BlockSpec(shape, lambda j: (0,) * len(shape))
    vec = lambda v: v.reshape(1, -1)
    return pl.pallas_call(
        _filter_kernel,
        grid=(n // tn,),
        in_specs=[
            full((length, emb)),
            full((emb, width)), full((1, width)), full((1, width)),
            full((width, width)), full((1, width)), full((1, width)),
            full((width, width)), full((1, width)), full((1, width)),
            pl.BlockSpec((width, tn), lambda j: (0, j)),
            pl.BlockSpec((1, tn), lambda j: (0, j)),
        ],
        out_specs=pl.BlockSpec((length, tn), lambda j: (0, j)),
        out_shape=jax.ShapeDtypeStruct((length, n), BF16),
        scratch_shapes=[pltpu.VMEM((length, width), F32)],
        compiler_params=_cparams("arbitrary"),
        name="hyena_filter",
    )(z, w1, vec(b1), vec(f1), w2, vec(b2), vec(f2), w3, vec(b3), vec(f3), w4, vec(deltas))


def _bmm_kernel(a_ref, x_ref, o_ref):
    o_ref[0] = jnp.dot(a_ref[...], x_ref[0], preferred_element_type=F32).astype(o_ref.dtype)


def block_dft(a, x, col0, ncols, out_dtype, tn=512, name="block_dft"):
    m, k = a.shape
    g = x.shape[0]
    c0 = col0 // tn
    return pl.pallas_call(
        _bmm_kernel,
        grid=(g, ncols // tn),
        in_specs=[
            pl.BlockSpec((m, k), lambda gi, j: (0, 0)),
            pl.BlockSpec((1, k, tn), lambda gi, j: (gi, 0, c0 + j)),
        ],
        out_specs=pl.BlockSpec((1, m, tn), lambda gi, j: (gi, 0, j)),
        out_shape=jax.ShapeDtypeStruct((g, m, ncols), out_dtype),
        compiler_params=_cparams("arbitrary", "arbitrary"),
        name=name,
    )(a, x)


CONV_HALO = 16


def _hyena_conv_kernel(fwd_ref, inv_ref, u_ref, g_ref, cw_ref, cb_ref, p_ref, q_ref, r_ref, bias_ref,
                       o_ref, uf_ref, us_ref, *, nblk, conv_u):
    s = u_ref.shape[1]
    tb = s // nblk

    def short_conv_block(ref, i, which):
        lo = max(i * tb - CONV_HALO, 0)
        hi = min((i + 1) * tb + CONV_HALO, s)
        x = ref[0, lo:hi].astype(F32)
        n = hi - lo
        row = lax.broadcasted_iota(jnp.int32, x.shape, 0) + lo
        prev = jnp.where(row == 0, 0.0, pltpu.roll(x, 1, 0))
        nxt = jnp.where(row == s - 1, 0.0, pltpu.roll(x, n - 1, 0))
        w = cw_ref[which]
        y = prev * w[0:1] + x * w[1:2] + nxt * w[2:3] + cb_ref[which]
        off = i * tb - lo
        return y[off:off + tb]

    for j in range(nblk):
        rows = slice(j * tb, (j + 1) * tb)
        ub = short_conv_block(u_ref, j, 0).astype(BF16) if conv_u else u_ref[0, rows]
        us_ref[rows] = ub
        uf_ref[j] = jnp.dot(fwd_ref[...], ub, preferred_element_type=F32)

    for i in range(nblk):
        re_acc = None
        im_acc = None
        for j in range(nblk):
            lag = i - j + nblk - 1
            re = uf_ref[j, :tb]
            im = uf_ref[j, tb:]
            p = p_ref[lag].astype(F32)
            q = q_ref[lag].astype(F32)
            r = r_ref[lag].astype(F32)
            tre = re * p - im * q
            tim = re * q + im * r
            re_acc = tre if re_acc is None else re_acc + tre
            im_acc = tim if im_acc is None else im_acc + tim
        yf = jnp.concatenate([re_acc, im_acc], axis=0).astype(BF16)
        y = jnp.dot(inv_ref[...], yf, preferred_element_type=F32)
        rows = slice(i * tb, (i + 1) * tb)
        gate = short_conv_block(g_ref, i, 1)
        o_ref[0, rows] = (gate * (y + us_ref[rows].astype(F32) * bias_ref[...])).astype(o_ref.dtype)


def hyena_conv(fwd, inv, u_arr, u_col0, conv_u, g_arr, g_col0, cw, cb, p, q, r, p_col0, bias, nblk, name, tn=256):
    nb, s, _ = u_arr.shape
    d = bias.shape[0]
    tb = s // nblk
    nlag = p.shape[0]
    uc0, gc0, pc0 = u_col0 // tn, g_col0 // tn, p_col0 // tn
    once = pl.Buffered(1)
    hspec = pl.BlockSpec((nlag, tb, tn), lambda ci, bi: (0, 0, pc0 + ci), pipeline_mode=once)
    return pl.pallas_call(
        functools.partial(_hyena_conv_kernel, nblk=nblk, conv_u=conv_u),
        grid=(d // tn, nb),
        in_specs=[
            pl.BlockSpec(fwd.shape, lambda ci, bi: (0, 0), pipeline_mode=once),
            pl.BlockSpec(inv.shape, lambda ci, bi: (0, 0), pipeline_mode=once),
            pl.BlockSpec((1, s, tn), lambda ci, bi: (bi, 0, uc0 + ci)),
            pl.BlockSpec((1, s, tn), lambda ci, bi: (bi, 0, gc0 + ci)),
            pl.BlockSpec((2, 3, tn), lambda ci, bi: (0, 0, ci)),
            pl.BlockSpec((2, 1, tn), lambda ci, bi: (0, 0, ci)),
            hspec, hspec, hspec,
            pl.BlockSpec((1, tn), lambda ci, bi: (0, ci)),
        ],
        out_specs=pl.BlockSpec((1, s, tn), lambda ci, bi: (bi, 0, ci)),
        out_shape=jax.ShapeDtypeStruct((nb, s, d), BF16),
        scratch_shapes=[pltpu.VMEM((nblk, 2 * tb, tn), F32), pltpu.VMEM((s, tn), BF16)],
        compiler_params=_cparams("arbitrary", "arbitrary"),
        name=name,
    )(fwd, inv, u_arr, g_arr, cw, cb, p, q, r, bias.reshape(1, d))


def _dft_matrices(tb):
    n = 2 * tb
    k = np.arange(tb, dtype=np.float64)[:, None]
    t = np.arange(tb, dtype=np.float64)[None, :]
    ang = 2.0 * np.pi * k * t / n
    fre = np.cos(ang)
    fim = -np.sin(ang)
    fim[0, :] = np.cos(np.pi * t[0])
    fwd = np.concatenate([fre, fim], axis=0)
    wre = np.full((tb, 1), 2.0)
    wre[0, 0] = 1.0
    ire = wre * np.cos(ang) / n
    iim = -2.0 * np.sin(ang) / n
    iim[0, :] = np.cos(np.pi * t[0]) / n
    inv = np.concatenate([ire, iim], axis=0).T
    sign = np.where(np.arange(tb) % 2 == 0, 1.0, -1.0)
    return jnp.asarray(fwd, BF16), jnp.asarray(inv, BF16), jnp.asarray(sign, F32)


def _fspec_kernel(ef_ref, eb_ref, tf_ref, tb_ref, s_ref, p_ref, q_ref, r_ref):
    nblk = ef_ref.shape[0]
    sgn = s_ref[...]
    shape = ef_ref.shape[2:]
    row0 = (lax.broadcasted_iota(jnp.int32, shape, 0) == 0) & (pl.program_id(1) == 0)

    def block(m):
        if m >= 0:
            return ef_ref[m, 0], ef_ref[m, 1], tf_ref[m]
        mu = -m
        tap = tb_ref[mu - 1]
        xre = eb_ref[mu - 1, 0] - tap
        xim = eb_ref[mu - 1, 1] - jnp.where(row0, tap, 0.0)
        b0 = tb_ref[mu] if mu < nblk else jnp.zeros_like(tap)
        return b0 + sgn * xre, jnp.where(row0, b0 + xim, -sgn * xim), b0

    prev = block(-nblk)
    for m in range(-(nblk - 1), nblk):
        cur = block(m)
        hre = cur[0] + sgn * (prev[0] - prev[2])
        him = cur[1] + sgn * (prev[1] - jnp.where(row0, prev[2], 0.0))
        lag = m + nblk - 1
        p_ref[lag] = hre.astype(p_ref.dtype)
        q_ref[lag] = jnp.where(row0, 0.0, him).astype(q_ref.dtype)
        r_ref[lag] = jnp.where(row0, him, hre).astype(r_ref.dtype)
        prev = cur


def filter_spectra(filt, fwd, sign, nblk, tb, tf=256, tn=256):
    d = D_MODEL
    ncol = filt.shape[1]
    fb = filt.reshape(nblk, tb, ncol)
    e = block_dft(fwd, fb, 0, ncol, F32, name="filter_dft").reshape(nblk, 2, tb, ncol)
    taps = fb[:, 0:1, :].astype(F32)
    nlag = 2 * nblk - 1
    per = d // tn
    espec = lambda direction: pl.BlockSpec(
        (nblk, 2, tf, tn), lambda o, fi, ci: (0, 0, fi, (2 * o + direction) * per + ci))
    tspec = lambda direction: pl.BlockSpec(
        (nblk, 1, tn), lambda o, fi, ci: (0, 0, (2 * o + direction) * per + ci))
    ospec = pl.BlockSpec((nlag, tf, tn), lambda o, fi, ci: (0, fi, o * per + ci))
    oshape = jax.ShapeDtypeStruct((nlag, tb, HYENA_ORDER * d), BF16)
    return pl.pallas_call(
        _fspec_kernel,
        grid=(HYENA_ORDER, tb // tf, per),
        in_specs=[espec(0), espec(1), tspec(0), tspec(1), pl.BlockSpec((tf, 1), lambda o, fi, ci: (fi, 0))],
        out_specs=[ospec, ospec, ospec],
        out_shape=[oshape, oshape, oshape],
        compiler_params=_cparams("arbitrary", "arbitrary", "arbitrary"),
        name="filter_spectra",
    )(e, e, taps, taps, sign.reshape(tb, 1))


def kernel(x, c, positions, ada_w, ada_b, norm_mix_g, norm_ffn_g, mla_w_down, mla_q_a_g, mla_kv_a_g, mla_w_uq, mla_w_ukv, mla_q_norm_g, mla_k_norm_g, mla_w_o, hy_w_in, hy_b_in, hy_conv_w, hy_conv_b, hy_f_w1, hy_f_b1, hy_f_freq1, hy_f_w2, hy_f_b2, hy_f_freq2, hy_f_w3, hy_f_b3, hy_f_freq3, hy_f_w4, hy_filt_bias, hy_w_out, moe_wg, moe_bg, moe_we, moe_be, moe_w_gate, moe_w_up, moe_w_down):
    batch, seq, d = x.shape
    t = batch * seq
    hh = MLA_HEADS
    xf = x.reshape(t, d)

    mod = adaln(c, ada_w, ada_b)

    def mods(i):
        return [mod[i, :, j * d:(j + 1) * d] for j in range(6)]

    def moe_layer(xin, i, sh2, sc2, g2):
        w_r = jnp.concatenate([moe_we[i], moe_wg[i], jnp.zeros((d, LANES - N_EXPERTS - N_GROUPS), F32)], axis=1)
        b_r = jnp.concatenate([moe_be[i], moe_bg[i], jnp.zeros((LANES - N_EXPERTS - N_GROUPS,), F32)]).reshape(1, LANES)
        tm = min(MOE_TILE, seq)
        h_ext, meta, counts = moe_router(xin, norm_ffn_g[i], sh2, sc2, w_r, b_r, seq, tm)
        return moe_experts(h_ext, meta, counts, moe_w_gate[i].astype(BF16), moe_w_up[i].astype(BF16),
                           moe_w_down[i].astype(BF16), xin, g2, seq, tm)

    sh1, sc1, g1, sh2, sc2, g2 = mods(0)
    lat_w = mla_w_down[0]
    w_dn = jnp.concatenate([lat_w, jnp.zeros((d, LAT_PAD - lat_w.shape[1]), F32)], axis=1).astype(BF16)
    lat = norm_mod_matmul(xf, norm_mix_g[0], sh1, sc1, w_dn, jnp.zeros((LAT_PAD,), F32), seq, F32, name="mla_down")

    inv_freq = 1.0 / (ROPE_THETA ** (jnp.arange(0, QK_ROPE, 2, dtype=F32) / QK_ROPE))
    ang = positions.astype(F32)[..., None] * inv_freq
    pad = jnp.zeros((batch, seq, LANES - QK_ROPE), F32)
    cos_t = jnp.concatenate([jnp.cos(ang), jnp.cos(ang), pad], axis=-1).reshape(t, LANES)
    sin_t = jnp.concatenate([jnp.sin(ang), jnp.sin(ang), pad], axis=-1).reshape(t, LANES)

    wq = mla_w_uq[0].reshape(Q_LORA, hh, QK_HEAD)
    wq = jnp.concatenate([wq, jnp.zeros((Q_LORA, hh, QK_PAD - QK_HEAD), F32)], axis=-1)
    wq = wq.reshape(Q_LORA, hh * QK_PAD).astype(BF16)
    wkv = mla_w_ukv[0].reshape(KV_LORA, hh, QK_NOPE + V_HEAD)
    wk = wkv[:, :, :QK_NOPE].reshape(KV_LORA, hh * QK_NOPE).astype(BF16)
    wv = wkv[:, :, QK_NOPE:].reshape(KV_LORA, hh * V_HEAD).astype(BF16)
    zpad = jnp.zeros((LANES - QK_ROPE,), F32)
    gq = jnp.concatenate([mla_q_norm_g[0], zpad]).reshape(1, QK_PAD)
    gkn = mla_k_norm_g[0][:QK_NOPE].reshape(1, LANES)
    gkp = jnp.concatenate([mla_k_norm_g[0][QK_NOPE:], zpad]).reshape(1, LANES)
    q, k, v = mla_qkv(lat, cos_t, sin_t, mla_q_a_g[0].reshape(1, Q_LORA), mla_kv_a_g[0].reshape(1, KV_LORA),
                      wq, wk, wv, gq, gkn, gkp, batch, seq)
    o = attention(q, k, v)
    xf = matmul_residual(o.reshape(t, hh * V_HEAD), mla_w_o[0].astype(BF16), xf, g1, seq, name="mla_out")
    xf = moe_layer(xf, 0, sh2, sc2, g2)

    sh1, sc1, g1, sh2, sc2, g2 = mods(1)
    nblk = CONV_BLOCKS
    tb = seq // nblk
    fwd, inv, sign = _dft_matrices(tb)

    tt = jnp.linspace(0.0, 1.0, seq, dtype=F32)[:, None]
    wfreq = 2.0 * math.pi * jnp.arange(seq, dtype=F32)[:, None] / seq
    fr = jnp.linspace(1e-4, FILTER_BANDS - 1, FILTER_BANDS, dtype=F32)[None, :]
    z = jnp.concatenate([tt, jnp.cos(fr * wfreq), -jnp.sin(fr * wfreq)], axis=-1)
    deltas = jnp.abs(jnp.linspace(math.log(FAST_DECAY) / DECAY_TARGET, math.log(SLOW_DECAY) / DECAY_TARGET, d, dtype=F32))
    filt = hyena_filter(z, hy_f_w1[0], hy_f_b1[0], hy_f_freq1[0], hy_f_w2[0], hy_f_b2[0], hy_f_freq2[0],
                        hy_f_w3[0], hy_f_b3[0], hy_f_freq3[0], hy_f_w4[0], jnp.tile(deltas, HYENA_ORDER * 2))
    p, qc, r = filter_spectra(filt, fwd, sign, nblk, tb)

    u = norm_mod_matmul(xf, norm_mix_g[1], sh1, sc1, hy_w_in[0].astype(BF16), hy_b_in[0], seq, BF16, name="hy_in")
    u3 = u.reshape(batch, seq, 3 * d)
    cw3 = hy_conv_w[0].reshape(3, 3, d)
    cb3 = hy_conv_b[0].reshape(3, 1, d)

    zsrc, zcol = u3, 2 * d
    for order in range(HYENA_ORDER):
        cw = jnp.stack([cw3[:, 2], cw3[:, order]], axis=0)
        cb = jnp.stack([cb3[2], cb3[order]], axis=0)
        zsrc = hyena_conv(fwd, inv, zsrc, zcol, order == 0, u3, order * d, cw, cb, p, qc, r, order * d,
                          hy_filt_bias[0, order], nblk, name=f"hyena_conv{order}")
        zcol = 0
    xf = matmul_residual(zsrc.reshape(t, d), hy_w_out[0].astype(BF16), xf, g1, seq, name="hy_out")
    xf = moe_layer(xf, 1, sh2, sc2, g2)
    return xf.reshape(batch, seq, d)
```

```python
import functools
import math

import jax
import jax.numpy as jnp
import numpy as np
from jax import lax
from jax.experimental import pallas as pl
from jax.experimental.pallas import tpu as pltpu

F32 = jnp.float32
BF16 = jnp.bfloat16
HIGHEST = lax.Precision.HIGHEST

D_MODEL = 1024
MLA_HEADS = 8
Q_LORA = 256
KV_LORA = 128
QK_NOPE = 128
QK_ROPE = 64
QK_HEAD = QK_NOPE + QK_ROPE
V_HEAD = 128
ROPE_HALF = QK_ROPE // 2
ROPE_THETA = 10000.0
HYENA_ORDER = 2
FILTER_EMB = 33
FILTER_BANDS = (FILTER_EMB - 1) // 2
FAST_DECAY = 0.3
SLOW_DECAY = 1.5
DECAY_TARGET = 1e-2
N_GROUPS = 4
EXPERTS_PER_GROUP = 4
N_EXPERTS = N_GROUPS * EXPERTS_PER_GROUP
D_EXPERT = 256
EPS = 1e-6

LANES = 128
QK_PAD = 2 * LANES
LAT_PAD = 512
CONV_BLOCKS = 8
ROUTER_LANES = 32
MOE_TILE = 1024
MOE_CHUNK = 128
MOE_SLAB = 256
VMEM_LIMIT = 56 * 1024 * 1024


def _cparams(*sem):
    return pltpu.CompilerParams(dimension_semantics=sem, vmem_limit_bytes=VMEM_LIMIT)


def _adaln_kernel(c_ref, w_ref, b_ref, o_ref):
    c = c_ref[...]
    ca = c * jax.nn.sigmoid(c)
    o_ref[0] = jnp.dot(ca, w_ref[0], precision=HIGHEST, preferred_element_type=F32) + b_ref[0]


def adaln(c, ada_w, ada_b, tn=1536):
    depth, d, n = ada_w.shape
    b = c.shape[0]
    return pl.pallas_call(
        _adaln_kernel,
        grid=(depth, n // tn),
        in_specs=[
            pl.BlockSpec((b, d), lambda i, j: (0, 0)),
            pl.BlockSpec((1, d, tn), lambda i, j: (i, 0, j)),
            pl.BlockSpec((1, 1, tn), lambda i, j: (i, 0, j)),
        ],
        out_specs=pl.BlockSpec((1, b, tn), lambda i, j: (i, 0, j)),
        out_shape=jax.ShapeDtypeStruct((depth, b, n), F32),
        compiler_params=_cparams("arbitrary", "arbitrary"),
        name="adaln",
    )(c, ada_w, ada_b.reshape(depth, 1, n))


def _norm_mod(x, g, sh, sc):
    ms = jnp.mean(x * x, axis=-1, keepdims=True)
    y = x * lax.rsqrt(ms + EPS) * g
    return y * (1.0 + sc) + sh


def _nmm_kernel(x_ref, g_ref, sh_ref, sc_ref, w_ref, b_ref, o_ref):
    h = _norm_mod(x_ref[...], g_ref[...], sh_ref[0], sc_ref[0])
    o = jnp.dot(h.astype(BF16), w_ref[...], preferred_element_type=F32) + b_ref[...]
    o_ref[...] = o.astype(o_ref.dtype)


def norm_mod_matmul(x, g, sh, sc, w, b, seq, out_dtype, tm=512, name="nmm"):
    t, d = x.shape
    n = w.shape[1]
    per = seq // tm
    nb = sh.shape[0]
    return pl.pallas_call(
        _nmm_kernel,
        grid=(t // tm,),
        in_specs=[
            pl.BlockSpec((tm, d), lambda i: (i, 0)),
            pl.BlockSpec((1, d), lambda i: (0, 0)),
            pl.BlockSpec((1, 1, d), lambda i: (i // per, 0, 0)),
            pl.BlockSpec((1, 1, d), lambda i: (i // per, 0, 0)),
            pl.BlockSpec((d, n), lambda i: (0, 0)),
            pl.BlockSpec((1, n), lambda i: (0, 0)),
        ],
        out_specs=pl.BlockSpec((tm, n), lambda i: (i, 0)),
        out_shape=jax.ShapeDtypeStruct((t, n), out_dtype),
        compiler_params=_cparams("arbitrary"),
        name=name,
    )(x, g.reshape(1, d), sh.reshape(nb, 1, d), sc.reshape(nb, 1, d), w, b.reshape(1, n))


def _mmres_kernel(a_ref, w_ref, x_ref, g_ref, o_ref):
    y = jnp.dot(a_ref[...], w_ref[...], preferred_element_type=F32)
    o_ref[...] = x_ref[...] + g_ref[0] * y


def matmul_residual(a, w, x, gate, seq, tm=512, name="mmres"):
    t, k = a.shape
    d = w.shape[1]
    per = seq // tm
    nb = gate.shape[0]
    return pl.pallas_call(
        _mmres_kernel,
        grid=(t // tm,),
        in_specs=[
            pl.BlockSpec((tm, k), lambda i: (i, 0)),
            pl.BlockSpec((k, d), lambda i: (0, 0)),
            pl.BlockSpec((tm, d), lambda i: (i, 0)),
            pl.BlockSpec((1, 1, d), lambda i: (i // per, 0, 0)),
        ],
        out_specs=pl.BlockSpec((tm, d), lambda i: (i, 0)),
        out_shape=jax.ShapeDtypeStruct((t, d), F32),
        compiler_params=_cparams("arbitrary"),
        name=name,
    )(a, w, x, gate.reshape(nb, 1, d))


def _rms(x, n):
    return x * lax.rsqrt(jnp.sum(x * x, axis=-1, keepdims=True) * (1.0 / n) + EPS)


def _qkv_kernel(x_ref, ng_ref, sh_ref, sc_ref, wdn_ref, cos_ref, sin_ref, qag_ref, kvag_ref, wq_ref, wk_ref, wv_ref,
                gq_ref, gkn_ref, gkp_ref, q_ref, k_ref, v_ref):
    h = _norm_mod(x_ref[...], ng_ref[...], sh_ref[0], sc_ref[0])
    lat = jnp.dot(h.astype(BF16), wdn_ref[...], preferred_element_type=F32)
    cq = lat[:, :Q_LORA]
    ckv = lat[:, Q_LORA:Q_LORA + KV_LORA]
    kpe = lat[:, Q_LORA + KV_LORA:]
    cqn = (_rms(cq, Q_LORA) * qag_ref[...]).astype(BF16)
    ckvn = (_rms(ckv, KV_LORA) * kvag_ref[...]).astype(BF16)
    qall = jnp.dot(cqn, wq_ref[...], preferred_element_type=F32)
    kall = jnp.dot(ckvn, wk_ref[...], preferred_element_type=F32)
    vall = jnp.dot(ckvn, wv_ref[...], preferred_element_type=F32)
    cos_t = cos_ref[...]
    sin_t = sin_ref[...]
    lane = lax.broadcasted_iota(jnp.int32, cos_t.shape, 1)

    def rope(pe):
        up = pltpu.roll(pe, ROPE_HALF, 1)
        down = pltpu.roll(pe, LANES - ROPE_HALF, 1)
        rot = jnp.where(lane < ROPE_HALF, -down, up)
        return pe * cos_t + rot * sin_t

    gq = gq_ref[...]
    kp = rope(_rms(kpe, QK_ROPE) * gkp_ref[...])
    scale = QK_HEAD ** -0.5 * math.log2(math.e)
    for h in range(MLA_HEADS):
        qh = qall[:, h * QK_PAD:(h + 1) * QK_PAD]
        qn = _rms(qh[:, :QK_NOPE], QK_NOPE) * gq[:, :QK_NOPE]
        qp = rope(_rms(qh[:, QK_NOPE:], QK_ROPE) * gq[:, QK_NOPE:])
        q_ref[0, h] = (jnp.concatenate([qn, qp], axis=-1) * scale).astype(q_ref.dtype)
        kn = _rms(kall[:, h * QK_NOPE:(h + 1) * QK_NOPE], QK_NOPE) * gkn_ref[...]
        k_ref[0, h] = jnp.concatenate([kn, kp], axis=-1).astype(k_ref.dtype)
        v_ref[0, h] = vall[:, h * V_HEAD:(h + 1) * V_HEAD].astype(v_ref.dtype)


def mla_qkv(x, norm_g, sh, sc, w_dn, cos_t, sin_t, q_a_g, kv_a_g, wq, wk, wv, gq, gkn, gkp, batch, seq, tm=256):
    t, d = x.shape
    per = seq // tm
    hh = MLA_HEADS
    full = lambda shape: pl.BlockSpec(shape, lambda i: (0,) * len(shape))
    return pl.pallas_call(
        _qkv_kernel,
        grid=(t // tm,),
        in_specs=[
            pl.BlockSpec((tm, d), lambda i: (i, 0)),
            full((1, d)),
            pl.BlockSpec((1, 1, d), lambda i: (i // per, 0, 0)),
            pl.BlockSpec((1, 1, d), lambda i: (i // per, 0, 0)),
            full(w_dn.shape),
            pl.BlockSpec((tm, LANES), lambda i: (i, 0)),
            pl.BlockSpec((tm, LANES), lambda i: (i, 0)),
            full((1, Q_LORA)), full((1, KV_LORA)),
            full(wq.shape), full(wk.shape), full(wv.shape),
            full((1, QK_PAD)), full((1, LANES)), full((1, LANES)),
        ],
        out_specs=[
            pl.BlockSpec((1, hh, tm, QK_PAD), lambda i: (i // per, 0, i % per, 0)),
            pl.BlockSpec((1, hh, tm, QK_PAD), lambda i: (i // per, 0, i % per, 0)),
            pl.BlockSpec((1, hh, tm, V_HEAD), lambda i: (i // per, 0, i % per, 0)),
        ],
        out_shape=[
            jax.ShapeDtypeStruct((batch, hh, seq, QK_PAD), BF16),
            jax.ShapeDtypeStruct((batch, hh, seq, QK_PAD), BF16),
            jax.ShapeDtypeStruct((batch, hh, seq, V_HEAD), BF16),
        ],
        compiler_params=_cparams("arbitrary"),
        name="mla_qkv",
    )(x, norm_g.reshape(1, d), sh.reshape(batch, 1, d), sc.reshape(batch, 1, d), w_dn,
      cos_t, sin_t, q_a_g, kv_a_g, wq, wk, wv, gq, gkn, gkp)


def _attn_kernel(q_ref, k_ref, v_ref, o_ref, *, tk):
    q = q_ref[0, 0]
    tq = q.shape[0]
    nk = k_ref.shape[2] // tk

    def body(j, carry):
        m, l, acc = carry
        start = pl.multiple_of(j * tk, tk)
        ks = k_ref[0, 0, pl.ds(start, tk), :]
        vs = v_ref[0, 0, pl.ds(start, tk), :]
        s = lax.dot_general(q, ks, (((1,), (1,)), ((), ())), preferred_element_type=F32)
        m_new = jnp.maximum(m, jnp.max(s, axis=-1, keepdims=True))
        p = jnp.exp2(s - m_new)
        alpha = jnp.exp2(m - m_new)
        l = alpha * l + jnp.sum(p, axis=-1, keepdims=True)
        acc = alpha * acc + jnp.dot(p.astype(BF16), vs, preferred_element_type=F32)
        return m_new, l, acc

    m0 = jnp.full((tq, 1), -jnp.inf, F32)
    l0 = jnp.zeros((tq, 1), F32)
    acc0 = jnp.zeros((tq, V_HEAD), F32)
    _, l, acc = lax.fori_loop(0, nk, body, (m0, l0, acc0), unroll=True)
    o_ref[0] = (acc / l).astype(o_ref.dtype)


def attention(q, k, v, tq=1024, tk=512):
    tq = min(tq, q.shape[2])
    tk = min(tk, q.shape[2])
    b, hh, s, _ = q.shape
    return pl.pallas_call(
        functools.partial(_attn_kernel, tk=tk),
        grid=(b, hh, s // tq),
        in_specs=[
            pl.BlockSpec((1, 1, tq, QK_PAD), lambda bi, hi, qi: (bi, hi, qi, 0)),
            pl.BlockSpec((1, 1, s, QK_PAD), lambda bi, hi, qi: (bi, hi, 0, 0)),
            pl.BlockSpec((1, 1, s, V_HEAD), lambda bi, hi, qi: (bi, hi, 0, 0)),
        ],
        out_specs=pl.BlockSpec((1, tq, V_HEAD), lambda bi, hi, qi: (bi, qi, hi)),
        out_shape=jax.ShapeDtypeStruct((b, s, hh * V_HEAD), BF16),
        compiler_params=_cparams("arbitrary", "arbitrary", "arbitrary"),
        name="attention",
    )(q, k, v)


def _router_kernel(x_ref, g_ref, sh_ref, sc_ref, w_ref, b_ref, tri_ref, h_ref, meta_ref, cnt_ref):
    tm, d = x_ref.shape
    h = _norm_mod(x_ref[...], g_ref[...], sh_ref[0], sc_ref[0])
    hb = h.astype(BF16)
    rem = h - hb.astype(F32)
    hm = rem.astype(BF16)
    hl = (rem - hm.astype(F32)).astype(BF16)
    wst = w_ref[...]
    part = (jnp.dot(hb, wst, preferred_element_type=F32) + jnp.dot(hm, wst, preferred_element_type=F32)
            + jnp.dot(hl, wst, preferred_element_type=F32))
    logits = (part + pltpu.roll(part, LANES - ROUTER_LANES, 1)
              + pltpu.roll(part, LANES - 2 * ROUTER_LANES, 1)) + b_ref[...]
    lane = lax.broadcasted_iota(jnp.int32, logits.shape, 1)
    lane_f = lane.astype(F32)
    neg = jnp.float32(-jnp.inf)
    big = jnp.float32(4 * LANES)
    gmask = (lane >= N_EXPERTS) & (lane < N_EXPERTS + N_GROUPS)
    gl = jnp.where(gmask, logits, neg)
    ge = jnp.exp(gl - jnp.max(gl, axis=-1, keepdims=True))
    gp = ge / jnp.sum(ge, axis=-1, keepdims=True)
    g_w = jnp.max(gp, axis=-1, keepdims=True)
    g_lane = jnp.min(jnp.where(gmask & (gp == g_w), lane_f, big), axis=-1, keepdims=True)
    grp = g_lane - N_EXPERTS
    e_lo = grp * EXPERTS_PER_GROUP
    emask = (lane_f >= e_lo) & (lane_f < e_lo + EXPERTS_PER_GROUP)
    el = jnp.where(emask, logits, neg)
    ee = jnp.exp(el - jnp.max(el, axis=-1, keepdims=True))
    ep = ee / jnp.sum(ee, axis=-1, keepdims=True)
    e1 = jnp.max(ep, axis=-1, keepdims=True)
    i1 = jnp.min(jnp.where(emask & (ep == e1), lane_f, big), axis=-1, keepdims=True)
    rest = emask & (lane_f != i1)
    ep2 = jnp.where(rest, ep, -1.0)
    e2 = jnp.max(ep2, axis=-1, keepdims=True)
    i2 = jnp.min(jnp.where(rest & (ep2 == e2), lane_f, big), axis=-1, keepdims=True)
    tot = e1 + e2
    w4 = jnp.where(lane_f == i1 - e_lo, g_w * (e1 / tot), jnp.where(lane_f == i2 - e_lo, g_w * (e2 / tot), 0.0))
    hi = w4.astype(BF16).astype(F32)
    mid = (w4 - hi).astype(BF16).astype(F32)
    lo = (w4 - hi - mid).astype(BF16).astype(F32)
    ext = hi + pltpu.roll(mid, EXPERTS_PER_GROUP, 1) + pltpu.roll(lo, 2 * EXPERTS_PER_GROUP, 1)
    h_ref[:, :d] = h.astype(h_ref.dtype)
    h_ref[:, d:] = ext.astype(h_ref.dtype)
    onehot = lane_f == grp
    ohf = jnp.where(onehot, 1.0, 0.0)
    counts = jnp.broadcast_to(jnp.sum(ohf, axis=0, keepdims=True), (8, LANES))
    starts = pltpu.roll(counts, 1, 1) + pltpu.roll(counts, 2, 1) + pltpu.roll(counts, 3, 1)
    rank = jnp.dot(tri_ref[...], ohf.astype(BF16), preferred_element_type=F32)
    pos = jnp.sum(jnp.where(onehot, rank - 1.0 + starts[0:1], 0.0), axis=-1, keepdims=True)
    meta_ref[...] = jnp.broadcast_to(pos, meta_ref.shape)
    cnt_ref[0] = counts


def moe_router(x, g, sh, sc, w_r, b_r, seq, tm):
    t, d = x.shape
    per = seq // tm
    nb = sh.shape[0]
    tri = jnp.tril(jnp.ones((tm, tm), BF16))
    return pl.pallas_call(
        _router_kernel,
        grid=(t // tm,),
        in_specs=[
            pl.BlockSpec((tm, d), lambda i: (i, 0)),
            pl.BlockSpec((1, d), lambda i: (0, 0)),
            pl.BlockSpec((1, 1, d), lambda i: (i // per, 0, 0)),
            pl.BlockSpec((1, 1, d), lambda i: (i // per, 0, 0)),
            pl.BlockSpec((d, LANES), lambda i: (0, 0)),
            pl.BlockSpec((1, LANES), lambda i: (0, 0)),
            pl.BlockSpec((tm, tm), lambda i: (0, 0)),
        ],
        out_specs=[
            pl.BlockSpec((tm, d + LANES), lambda i: (i, 0)),
            pl.BlockSpec((tm, LANES), lambda i: (i, 0)),
            pl.BlockSpec((1, 8, LANES), lambda i: (i, 0, 0)),
        ],
        out_shape=[
            jax.ShapeDtypeStruct((t, d + LANES), BF16),
            jax.ShapeDtypeStruct((t, LANES), F32),
            jax.ShapeDtypeStruct((t // tm, 8, LANES), F32),
        ],
        compiler_params=_cparams("arbitrary"),
        name="moe_router",
    )(x, g.reshape(1, d), sh.reshape(nb, 1, d), sc.reshape(nb, 1, d), w_r, b_r, tri)


def _moe_kernel(st_ref, en_ref, h_ref, meta_ref, posr_ref, wg_ref, wu_ref, wd_ref, x_ref, gate_ref, o_ref,
                hs_ref, y_ref):
    i = pl.program_id(0)
    g = pl.program_id(1)
    tm, d = x_ref.shape
    slab = min(MOE_SLAB, tm)

    @pl.when(g == 0)
    def _():
        pos_row = posr_ref[0]
        for r in range(0, tm, slab):
            rows = lax.broadcasted_iota(jnp.int32, (slab, tm), 0).astype(F32) + float(r)
            perm = jnp.where(rows == pos_row, 1.0, 0.0).astype(BF16)
            hs_ref[r:r + slab] = jnp.dot(perm, h_ref[...], preferred_element_type=F32).astype(BF16)
        y_ref[...] = jnp.zeros_like(y_ref)

    start = st_ref[i * N_GROUPS + g]
    end = en_ref[i * N_GROUPS + g]
    shift = MOE_CHUNK.bit_length() - 1
    c_lo = lax.shift_right_logical(start, shift)
    c_hi = jnp.where(end > start, lax.shift_right_logical(end + (MOE_CHUNK - 1), shift), c_lo)

    def chunk(c, carry):
        r0 = pl.multiple_of(c * MOE_CHUNK, MOE_CHUNK)
        hx = hs_ref[pl.ds(r0, MOE_CHUNK), :]
        hs = hx[:, :d]
        ext = hx[:, d:].astype(F32)
        w4 = (ext + pltpu.roll(ext, LANES - EXPERTS_PER_GROUP, 1)
              + pltpu.roll(ext, LANES - 2 * EXPERTS_PER_GROUP, 1))
        rowid = r0 + lax.broadcasted_iota(jnp.int32, w4.shape, 0)
        w4 = jnp.where((rowid >= start) & (rowid < end), w4, 0.0)
        lane = lax.broadcasted_iota(jnp.int32, w4.shape, 1)
        acc = jnp.zeros((MOE_CHUNK, d), F32)
        for j in range(EXPERTS_PER_GROUP):
            a = jnp.dot(hs, wg_ref[j], preferred_element_type=F32)
            u = jnp.dot(hs, wu_ref[j], preferred_element_type=F32)
            wj = jnp.sum(jnp.where(lane == j, w4, 0.0), axis=-1, keepdims=True)
            act = (a * jax.nn.sigmoid(a)) * u * wj
            acc = acc + jnp.dot(act.astype(BF16), wd_ref[j], preferred_element_type=F32)
        y_ref[pl.ds(r0, MOE_CHUNK), :] += acc
        return carry

    lax.fori_loop(c_lo, c_hi, chunk, 0)

    @pl.when(g == N_GROUPS - 1)
    def _():
        yb = y_ref[...].astype(BF16)
        for r in range(0, tm, slab):
            pos_col = meta_ref[r:r + slab, 0:1]
            cols = lax.broadcasted_iota(jnp.int32, (slab, tm), 1).astype(F32)
            perm_t = jnp.where(cols == pos_col, 1.0, 0.0).astype(BF16)
            y = jnp.dot(perm_t, yb, preferred_element_type=F32)
            o_ref[r:r + slab] = x_ref[r:r + slab] + gate_ref[0] * y


def moe_experts(h_ext, meta, counts, wg, wu, wd, x, gate, seq, tm):
    t, d = x.shape
    f = wg.shape[2]
    per = seq // tm
    nb = gate.shape[0]
    ntiles = t // tm
    cnt = counts[:, 0, :N_GROUPS].astype(jnp.int32)
    ends = jnp.cumsum(cnt, axis=1)
    starts = ends - cnt
    pos_row = meta[:, 0].reshape(ntiles, 1, tm)
    epg = EXPERTS_PER_GROUP
    grid_spec = pltpu.PrefetchScalarGridSpec(
        num_scalar_prefetch=2,
        grid=(ntiles, N_GROUPS),
        in_specs=[
            pl.BlockSpec((tm, d + LANES), lambda i, g, st, en: (i, 0)),
            pl.BlockSpec((tm, LANES), lambda i, g, st, en: (i, 0)),
            pl.BlockSpec((1, 1, tm), lambda i, g, st, en: (i, 0, 0)),
            pl.BlockSpec((epg, d, f), lambda i, g, st, en: (g, 0, 0)),
            pl.BlockSpec((epg, d, f), lambda i, g, st, en: (g, 0, 0)),
            pl.BlockSpec((epg, f, d), lambda i, g, st, en: (g, 0, 0)),
            pl.BlockSpec((tm, d), lambda i, g, st, en: (i, 0)),
            pl.BlockSpec((1, 1, d), lambda i, g, st, en: (i // per, 0, 0)),
        ],
        out_specs=pl.BlockSpec((tm, d), lambda i, g, st, en: (i, 0)),
        scratch_shapes=[pltpu.VMEM((tm, d + LANES), BF16), pltpu.VMEM((tm, d), F32)],
    )
    return pl.pallas_call(
        _moe_kernel,
        grid_spec=grid_spec,
        out_shape=jax.ShapeDtypeStruct((t, d), F32),
        compiler_params=_cparams("arbitrary", "arbitrary"),
        name="moe_experts",
    )(starts.reshape(-1), ends.reshape(-1), h_ext, meta, pos_row, wg, wu, wd, x, gate.reshape(nb, 1, d))


def _filter_kernel(z_ref, w1_ref, b1_ref, f1_ref, w2_ref, b2_ref, f2_ref, w3_ref, b3_ref, f3_ref,
                   w4_ref, delta_ref, o_ref, hdn_ref):
    dot = functools.partial(jnp.dot, precision=HIGHEST, preferred_element_type=F32)

    @pl.when(pl.program_id(0) == 0)
    def _():
        hdn = jnp.sin(f1_ref[...] * (dot(z_ref[...], w1_ref[...]) + b1_ref[...]))
        hdn = jnp.sin(f2_ref[...] * (dot(hdn, w2_ref[...]) + b2_ref[...]))
        hdn_ref[...] = jnp.sin(f3_ref[...] * (dot(hdn, w3_ref[...]) + b3_ref[...]))

    h = dot(hdn_ref[...], w4_ref[...])
    t = z_ref[:, 0:1]
    h = h * jnp.exp(-t * delta_ref[...])
    o_ref[...] = (h / (jnp.sum(jnp.abs(h), axis=0, keepdims=True) + EPS)).astype(o_ref.dtype)


def hyena_filter(z, w1, b1, f1, w2, b2, f2, w3, b3, f3, w4, deltas, tn=256):
    length, emb = z.shape
    width = w2.shape[0]
    n = w4.shape[1]
    full = lambda shape: pl.BlockSpec(shape, lambda j: (0,) * len(shape))
    vec = lambda v: v.reshape(1, -1)
    return pl.pallas_call(
        _filter_kernel,
        grid=(n // tn,),
        in_specs=[
            full((length, emb)),
            full((emb, width)), full((1, width)), full((1, width)),
            full((width, width)), full((1, width)), full((1, width)),
            full((width, width)), full((1, width)), full((1, width)),
            pl.BlockSpec((width, tn), lambda j: (0, j)),
            pl.BlockSpec((1, tn), lambda j: (0, j)),
        ],
        out_specs=pl.BlockSpec((length, tn), lambda j: (0, j)),
        out_shape=jax.ShapeDtypeStruct((length, n), BF16),
        scratch_shapes=[pltpu.VMEM((length, width), F32)],
        compiler_params=_cparams("arbitrary"),
        name="hyena_filter",
    )(z, w1, vec(b1), vec(f1), w2, vec(b2), vec(f2), w3, vec(b3), vec(f3), w4, vec(deltas))


def _bmm_kernel(a_ref, x_ref, o_ref):
    o_ref[0] = jnp.dot(a_ref[...], x_ref[0], preferred_element_type=F32).astype(o_ref.dtype)


def block_dft(a, x, col0, ncols, out_dtype, tn=512, name="block_dft"):
    m, k = a.shape
    g = x.shape[0]
    c0 = col0 // tn
    return pl.pallas_call(
        _bmm_kernel,
        grid=(g, ncols // tn),
        in_specs=[
            pl.BlockSpec((m, k), lambda gi, j: (0, 0)),
            pl.BlockSpec((1, k, tn), lambda gi, j: (gi, 0, c0 + j)),
        ],
        out_specs=pl.BlockSpec((1, m, tn), lambda gi, j: (gi, 0, j)),
        out_shape=jax.ShapeDtypeStruct((g, m, ncols), out_dtype),
        compiler_params=_cparams("arbitrary", "arbitrary"),
        name=name,
    )(a, x)


CONV_HALO = 16


def _hyena_conv_kernel(fwd_ref, inv_ref, u_ref, g_ref, cw_ref, cb_ref, p_ref, q_ref, r_ref, bias_ref,
                       o_ref, uf_ref, us_ref, *, nblk, conv_u):
    s = u_ref.shape[1]
    tb = s // nblk

    def short_conv_block(ref, i, which):
        lo = max(i * tb - CONV_HALO, 0)
        hi = min((i + 1) * tb + CONV_HALO, s)
        x = ref[0, lo:hi].astype(F32)
        n = hi - lo
        row = lax.broadcasted_iota(jnp.int32, x.shape, 0) + lo
        prev = jnp.where(row == 0, 0.0, pltpu.roll(x, 1, 0))
        nxt = jnp.where(row == s - 1, 0.0, pltpu.roll(x, n - 1, 0))
        w = cw_ref[which]
        y = prev * w[0:1] + x * w[1:2] + nxt * w[2:3] + cb_ref[which]
        off = i * tb - lo
        return y[off:off + tb]

    for j in range(nblk):
        rows = slice(j * tb, (j + 1) * tb)
        ub = short_conv_block(u_ref, j, 0).astype(BF16) if conv_u else u_ref[0, rows]
        us_ref[rows] = ub
        uf_ref[j] = jnp.dot(fwd_ref[...], ub, preferred_element_type=F32).astype(uf_ref.dtype)

    for i in range(nblk):
        re_acc = None
        im_acc = None
        for j in range(nblk):
            lag = i - j + nblk - 1
            re = uf_ref[j, :tb]
            im = uf_ref[j, tb:]
            p = p_ref[lag]
            q = q_ref[lag]
            r = r_ref[lag]
            tre = re * p - im * q
            tim = re * q + im * r
            re_acc = tre if re_acc is None else re_acc + tre
            im_acc = tim if im_acc is None else im_acc + tim
        yf = jnp.concatenate([re_acc, im_acc], axis=0)
        y = jnp.dot(inv_ref[...], yf, preferred_element_type=F32)
        rows = slice(i * tb, (i + 1) * tb)
        gate = short_conv_block(g_ref, i, 1)
        o_ref[0, rows] = (gate * (y + us_ref[rows].astype(F32) * bias_ref[...])).astype(o_ref.dtype)


def hyena_conv(fwd, inv, u_arr, u_col0, conv_u, g_arr, g_col0, cw, cb, p, q, r, p_col0, bias, nblk, name, tn=256):
    nb, s, _ = u_arr.shape
    d = bias.shape[0]
    tb = s // nblk
    nlag = p.shape[0]
    uc0, gc0, pc0 = u_col0 // tn, g_col0 // tn, p_col0 // tn
    once = pl.Buffered(1)
    hspec = pl.BlockSpec((nlag, tb, tn), lambda ci, bi: (0, 0, pc0 + ci), pipeline_mode=once)
    return pl.pallas_call(
        functools.partial(_hyena_conv_kernel, nblk=nblk, conv_u=conv_u),
        grid=(d // tn, nb),
        in_specs=[
            pl.BlockSpec(fwd.shape, lambda ci, bi: (0, 0), pipeline_mode=once),
            pl.BlockSpec(inv.shape, lambda ci, bi: (0, 0), pipeline_mode=once),
            pl.BlockSpec((1, s, tn), lambda ci, bi: (bi, 0, uc0 + ci)),
            pl.BlockSpec((1, s, tn), lambda ci, bi: (bi, 0, gc0 + ci)),
            pl.BlockSpec((2, 3, tn), lambda ci, bi: (0, 0, ci)),
            pl.BlockSpec((2, 1, tn), lambda ci, bi: (0, 0, ci)),
            hspec, hspec, hspec,
            pl.BlockSpec((1, tn), lambda ci, bi: (0, ci)),
        ],
        out_specs=pl.BlockSpec((1, s, tn), lambda ci, bi: (bi, 0, ci)),
        out_shape=jax.ShapeDtypeStruct((nb, s, d), BF16),
        scratch_shapes=[pltpu.VMEM((nblk, 2 * tb, tn), BF16), pltpu.VMEM((s, tn), BF16)],
        compiler_params=_cparams("arbitrary", "arbitrary"),
        name=name,
    )(fwd, inv, u_arr, g_arr, cw, cb, p, q, r, bias.reshape(1, d))


def _dft_matrices(tb):
    n = 2 * tb
    k = np.arange(tb, dtype=np.float64)[:, None]
    t = np.arange(tb, dtype=np.float64)[None, :]
    ang = 2.0 * np.pi * k * t / n
    fre = np.cos(ang)
    fim = -np.sin(ang)
    fim[0, :] = np.cos(np.pi * t[0])
    fwd = np.concatenate([fre, fim], axis=0)
    wre = np.full((tb, 1), 2.0)
    wre[0, 0] = 1.0
    ire = wre * np.cos(ang) / n
    iim = -2.0 * np.sin(ang) / n
    iim[0, :] = np.cos(np.pi * t[0]) / n
    inv = np.concatenate([ire, iim], axis=0).T
    sign = np.where(np.arange(tb) % 2 == 0, 1.0, -1.0)
    return jnp.asarray(fwd, BF16), jnp.asarray(inv, BF16), jnp.asarray(sign, F32)


def _fspec_kernel(ef_ref, eb_ref, tf_ref, tb_ref, s_ref, p_ref, q_ref, r_ref):
    nblk = ef_ref.shape[0]
    sgn = s_ref[...]
    shape = ef_ref.shape[2:]
    row0 = (lax.broadcasted_iota(jnp.int32, shape, 0) == 0) & (pl.program_id(1) == 0)

    def block(m):
        if m >= 0:
            return ef_ref[m, 0], ef_ref[m, 1], tf_ref[m]
        mu = -m
        tap = tb_ref[mu - 1]
        xre = eb_ref[mu - 1, 0] - tap
        xim = eb_ref[mu - 1, 1] - jnp.where(row0, tap, 0.0)
        b0 = tb_ref[mu] if mu < nblk else jnp.zeros_like(tap)
        return b0 + sgn * xre, jnp.where(row0, b0 + xim, -sgn * xim), b0

    prev = block(-nblk)
    for m in range(-(nblk - 1), nblk):
        cur = block(m)
        hre = cur[0] + sgn * (prev[0] - prev[2])
        him = cur[1] + sgn * (prev[1] - jnp.where(row0, prev[2], 0.0))
        lag = m + nblk - 1
        p_ref[lag] = hre.astype(p_ref.dtype)
        q_ref[lag] = jnp.where(row0, 0.0, him).astype(q_ref.dtype)
        r_ref[lag] = jnp.where(row0, him, hre).astype(r_ref.dtype)
        prev = cur


def filter_spectra(filt, fwd, sign, nblk, tb, tf=256, tn=256):
    d = D_MODEL
    tf = min(tf, tb)
    ncol = filt.shape[1]
    fb = filt.reshape(nblk, tb, ncol)
    e = block_dft(fwd, fb, 0, ncol, F32, name="filter_dft").reshape(nblk, 2, tb, ncol)
    taps = fb[:, 0:1, :].astype(F32)
    nlag = 2 * nblk - 1
    per = d // tn
    espec = lambda direction: pl.BlockSpec(
        (nblk, 2, tf, tn), lambda o, fi, ci: (0, 0, fi, (2 * o + direction) * per + ci))
    tspec = lambda direction: pl.BlockSpec(
        (nblk, 1, tn), lambda o, fi, ci: (0, 0, (2 * o + direction) * per + ci))
    ospec = pl.BlockSpec((nlag, tf, tn), lambda o, fi, ci: (0, fi, o * per + ci))
    oshape = jax.ShapeDtypeStruct((nlag, tb, HYENA_ORDER * d), BF16)
    return pl.pallas_call(
        _fspec_kernel,
        grid=(HYENA_ORDER, tb // tf, per),
        in_specs=[espec(0), espec(1), tspec(0), tspec(1), pl.BlockSpec((tf, 1), lambda o, fi, ci: (fi, 0))],
        out_specs=[ospec, ospec, ospec],
        out_shape=[oshape, oshape, oshape],
        compiler_params=_cparams("arbitrary", "arbitrary", "arbitrary"),
        name="filter_spectra",
    )(e, e, taps, taps, sign.reshape(tb, 1))


def kernel(x, c, positions, ada_w, ada_b, norm_mix_g, norm_ffn_g, mla_w_down, mla_q_a_g, mla_kv_a_g, mla_w_uq, mla_w_ukv, mla_q_norm_g, mla_k_norm_g, mla_w_o, hy_w_in, hy_b_in, hy_conv_w, hy_conv_b, hy_f_w1, hy_f_b1, hy_f_freq1, hy_f_w2, hy_f_b2, hy_f_freq2, hy_f_w3, hy_f_b3, hy_f_freq3, hy_f_w4, hy_filt_bias, hy_w_out, moe_wg, moe_bg, moe_we, moe_be, moe_w_gate, moe_w_up, moe_w_down):
    batch, seq, d = x.shape
    t = batch * seq
    hh = MLA_HEADS
    xf = x.reshape(t, d)

    mod = adaln(c, ada_w, ada_b)

    def mods(i):
        return [mod[i, :, j * d:(j + 1) * d] for j in range(6)]

    def moe_layer(xin, i, sh2, sc2, g2):
        w_r = jnp.concatenate([moe_we[i], moe_wg[i], jnp.zeros((d, ROUTER_LANES - N_EXPERTS - N_GROUPS), F32)], axis=1)
        w_hi = w_r.astype(BF16)
        w_rem = w_r - w_hi.astype(F32)
        w_mid = w_rem.astype(BF16)
        w_lo = (w_rem - w_mid.astype(F32)).astype(BF16)
        w_r = jnp.concatenate([w_hi, w_mid, w_lo, jnp.zeros((d, LANES - 3 * ROUTER_LANES), BF16)], axis=1)
        b_r = jnp.concatenate([moe_be[i], moe_bg[i], jnp.zeros((LANES - N_EXPERTS - N_GROUPS,), F32)]).reshape(1, LANES)
        tm = min(MOE_TILE, seq)
        h_ext, meta, counts = moe_router(xin, norm_ffn_g[i], sh2, sc2, w_r, b_r, seq, tm)
        return moe_experts(h_ext, meta, counts, moe_w_gate[i].astype(BF16), moe_w_up[i].astype(BF16),
                           moe_w_down[i].astype(BF16), xin, g2, seq, tm)

    sh1, sc1, g1, sh2, sc2, g2 = mods(0)
    lat_w = mla_w_down[0]
    w_dn = jnp.concatenate([lat_w, jnp.zeros((d, LAT_PAD - lat_w.shape[1]), F32)], axis=1).astype(BF16)

    inv_freq = 1.0 / (ROPE_THETA ** (jnp.arange(0, QK_ROPE, 2, dtype=F32) / QK_ROPE))
    ang = positions.astype(F32)[..., None] * inv_freq
    pad = jnp.zeros((batch, seq, LANES - QK_ROPE), F32)
    cos_t = jnp.concatenate([jnp.cos(ang), jnp.cos(ang), pad], axis=-1).reshape(t, LANES)
    sin_t = jnp.concatenate([jnp.sin(ang), jnp.sin(ang), pad], axis=-1).reshape(t, LANES)

    wq = mla_w_uq[0].reshape(Q_LORA, hh, QK_HEAD)
    wq = jnp.concatenate([wq, jnp.zeros((Q_LORA, hh, QK_PAD - QK_HEAD), F32)], axis=-1)
    wq = wq.reshape(Q_LORA, hh * QK_PAD).astype(BF16)
    wkv = mla_w_ukv[0].reshape(KV_LORA, hh, QK_NOPE + V_HEAD)
    wk = wkv[:, :, :QK_NOPE].reshape(KV_LORA, hh * QK_NOPE).astype(BF16)
    wv = wkv[:, :, QK_NOPE:].reshape(KV_LORA, hh * V_HEAD).astype(BF16)
    zpad = jnp.zeros((LANES - QK_ROPE,), F32)
    gq = jnp.concatenate([mla_q_norm_g[0], zpad]).reshape(1, QK_PAD)
    gkn = mla_k_norm_g[0][:QK_NOPE].reshape(1, LANES)
    gkp = jnp.concatenate([mla_k_norm_g[0][QK_NOPE:], zpad]).reshape(1, LANES)
    q, k, v = mla_qkv(xf, norm_mix_g[0], sh1, sc1, w_dn, cos_t, sin_t, mla_q_a_g[0].reshape(1, Q_LORA), mla_kv_a_g[0].reshape(1, KV_LORA),
                      wq, wk, wv, gq, gkn, gkp, batch, seq)
    o = attention(q, k, v)
    xf = matmul_residual(o.reshape(t, hh * V_HEAD), mla_w_o[0].astype(BF16), xf, g1, seq, name="mla_out")
    xf = moe_layer(xf, 0, sh2, sc2, g2)

    sh1, sc1, g1, sh2, sc2, g2 = mods(1)
    nblk = CONV_BLOCKS
    tb = seq // nblk
    fwd, inv, sign = _dft_matrices(tb)

    tt = jnp.linspace(0.0, 1.0, seq, dtype=F32)[:, None]
    wfreq = 2.0 * math.pi * jnp.arange(seq, dtype=F32)[:, None] / seq
    fr = jnp.linspace(1e-4, FILTER_BANDS - 1, FILTER_BANDS, dtype=F32)[None, :]
    z = jnp.concatenate([tt, jnp.cos(fr * wfreq), -jnp.sin(fr * wfreq)], axis=-1)
    deltas = jnp.abs(jnp.linspace(math.log(FAST_DECAY) / DECAY_TARGET, math.log(SLOW_DECAY) / DECAY_TARGET, d, dtype=F32))
    filt = hyena_filter(z, hy_f_w1[0], hy_f_b1[0], hy_f_freq1[0], hy_f_w2[0], hy_f_b2[0], hy_f_freq2[0],
                        hy_f_w3[0], hy_f_b3[0], hy_f_freq3[0], hy_f_w4[0], jnp.tile(deltas, HYENA_ORDER * 2))
    p, qc, r = filter_spectra(filt, fwd, sign, nblk, tb)

    u = norm_mod_matmul(xf, norm_mix_g[1], sh1, sc1, hy_w_in[0].astype(BF16), hy_b_in[0], seq, BF16, name="hy_in")
    u3 = u.reshape(batch, seq, 3 * d)
    cw3 = hy_conv_w[0].reshape(3, 3, d)
    cb3 = hy_conv_b[0].reshape(3, 1, d)

    zsrc, zcol = u3, 2 * d
    for order in range(HYENA_ORDER):
        cw = jnp.stack([cw3[:, 2], cw3[:, order]], axis=0)
        cb = jnp.stack([cb3[2], cb3[order]], axis=0)
        zsrc = hyena_conv(fwd, inv, zsrc, zcol, order == 0, u3, order * d, cw, cb, p, qc, r, order * d,
                          hy_filt_bias[0, order], nblk, name=f"hyena_conv{order}")
        zcol = 0
    xf = matmul_residual(zsrc.reshape(t, d), hy_w_out[0].astype(BF16), xf, g1, seq, name="hy_out")
    xf = moe_layer(xf, 1, sh2, sc2, g2)
    return xf.reshape(batch, seq, d)
```

```python
import functools
import math

import jax
import jax.numpy as jnp
import numpy as np
from jax import lax
from jax.experimental import pallas as pl
from jax.experimental.pallas import tpu as pltpu

F32 = jnp.float32
BF16 = jnp.bfloat16
HIGHEST = lax.Precision.HIGHEST

D_MODEL = 1024
MLA_HEADS = 8
Q_LORA = 256
KV_LORA = 128
QK_NOPE = 128
QK_ROPE = 64
QK_HEAD = QK_NOPE + QK_ROPE
V_HEAD = 128
ROPE_HALF = QK_ROPE // 2
ROPE_THETA = 10000.0
HYENA_ORDER = 2
FILTER_EMB = 33
FILTER_BANDS = (FILTER_EMB - 1) // 2
FAST_DECAY = 0.3
SLOW_DECAY = 1.5
DECAY_TARGET = 1e-2
N_GROUPS = 4
EXPERTS_PER_GROUP = 4
N_EXPERTS = N_GROUPS * EXPERTS_PER_GROUP
D_EXPERT = 256
EPS = 1e-6

LANES = 128
QK_PAD = 2 * LANES
CONV_BLOCKS = 8
ROUTER_LANES = 32
MOE_TILE = 1024
MOE_CHUNK = 128
MOE_SLAB = 256
VMEM_LIMIT = 56 * 1024 * 1024


def _cparams(*sem):
    return pltpu.CompilerParams(dimension_semantics=sem, vmem_limit_bytes=VMEM_LIMIT)


def _adaln_kernel(c_ref, w_ref, b_ref, o_ref):
    c = c_ref[...]
    ca = c * jax.nn.sigmoid(c)
    o_ref[0] = jnp.dot(ca, w_ref[0], precision=HIGHEST, preferred_element_type=F32) + b_ref[0]


def adaln(c, ada_w, ada_b, tn=1536):
    depth, d, n = ada_w.shape
    b = c.shape[0]
    return pl.pallas_call(
        _adaln_kernel,
        grid=(depth, n // tn),
        in_specs=[
            pl.BlockSpec((b, d), lambda i, j: (0, 0)),
            pl.BlockSpec((1, d, tn), lambda i, j: (i, 0, j)),
            pl.BlockSpec((1, 1, tn), lambda i, j: (i, 0, j)),
        ],
        out_specs=pl.BlockSpec((1, b, tn), lambda i, j: (i, 0, j)),
        out_shape=jax.ShapeDtypeStruct((depth, b, n), F32),
        compiler_params=_cparams("arbitrary", "arbitrary"),
        name="adaln",
    )(c, ada_w, ada_b.reshape(depth, 1, n))


def _norm_mod(x, g, sh, sc):
    ms = jnp.mean(x * x, axis=-1, keepdims=True)
    y = x * lax.rsqrt(ms + EPS) * g
    return y * (1.0 + sc) + sh


def _nmm_kernel(x_ref, g_ref, sh_ref, sc_ref, w_ref, b_ref, o_ref):
    h = _norm_mod(x_ref[...], g_ref[...], sh_ref[0], sc_ref[0])
    o = jnp.dot(h.astype(BF16), w_ref[...], preferred_element_type=F32) + b_ref[...]
    o_ref[...] = o.astype(o_ref.dtype)


def norm_mod_matmul(x, g, sh, sc, w, b, seq, out_dtype, tm=512, name="nmm"):
    t, d = x.shape
    n = w.shape[1]
    per = seq // tm
    nb = sh.shape[0]
    return pl.pallas_call(
        _nmm_kernel,
        grid=(t // tm,),
        in_specs=[
            pl.BlockSpec((tm, d), lambda i: (i, 0)),
            pl.BlockSpec((1, d), lambda i: (0, 0)),
            pl.BlockSpec((1, 1, d), lambda i: (i // per, 0, 0)),
            pl.BlockSpec((1, 1, d), lambda i: (i // per, 0, 0)),
            pl.BlockSpec((d, n), lambda i: (0, 0)),
            pl.BlockSpec((1, n), lambda i: (0, 0)),
        ],
        out_specs=pl.BlockSpec((tm, n), lambda i: (i, 0)),
        out_shape=jax.ShapeDtypeStruct((t, n), out_dtype),
        compiler_params=_cparams("arbitrary"),
        name=name,
    )(x, g.reshape(1, d), sh.reshape(nb, 1, d), sc.reshape(nb, 1, d), w, b.reshape(1, n))


def _rms(x, n):
    return x * lax.rsqrt(jnp.sum(x * x, axis=-1, keepdims=True) * (1.0 / n) + EPS)


def _qkv_kernel(x_ref, ng_ref, sh_ref, sc_ref, wdn_ref, cos_ref, sin_ref, qag_ref, kvag_ref, wq_ref, wk_ref, wv_ref,
                gq_ref, gkn_ref, gkp_ref, q_ref, k_ref, v_ref):
    h = _norm_mod(x_ref[...], ng_ref[...], sh_ref[0], sc_ref[0])
    lat = jnp.dot(h.astype(BF16), wdn_ref[...], preferred_element_type=F32)
    cq = lat[:, :Q_LORA]
    ckv = lat[:, Q_LORA:Q_LORA + KV_LORA]
    kpe = lat[:, Q_LORA + KV_LORA:]
    cqn = (_rms(cq, Q_LORA) * qag_ref[...]).astype(BF16)
    ckvn = (_rms(ckv, KV_LORA) * kvag_ref[...]).astype(BF16)
    qall = jnp.dot(cqn, wq_ref[...], preferred_element_type=F32)
    kall = jnp.dot(ckvn, wk_ref[...], preferred_element_type=F32)
    vall = jnp.dot(ckvn, wv_ref[...], preferred_element_type=F32)
    cos_t = cos_ref[...]
    sin_t = sin_ref[...]

    def rope(pe):
        return pe * cos_t + pltpu.roll(pe, ROPE_HALF, 1) * sin_t

    gq = gq_ref[...]
    kp = rope(_rms(kpe, 2 * QK_ROPE) * gkp_ref[...])
    scale = QK_HEAD ** -0.5 * math.log2(math.e)
    for h in range(MLA_HEADS):
        qh = qall[:, h * QK_PAD:(h + 1) * QK_PAD]
        qn = _rms(qh[:, :QK_NOPE], QK_NOPE) * gq[:, :QK_NOPE]
        qp = rope(_rms(qh[:, QK_NOPE:], 2 * QK_ROPE) * gq[:, QK_NOPE:])
        q_ref[0, h] = (jnp.concatenate([qn, qp], axis=-1) * scale).astype(q_ref.dtype)
        kn = _rms(kall[:, h * QK_NOPE:(h + 1) * QK_NOPE], QK_NOPE) * gkn_ref[...]
        k_ref[0, h] = jnp.concatenate([kn, kp], axis=-1).astype(k_ref.dtype)
        v_ref[0, h] = vall[:, h * V_HEAD:(h + 1) * V_HEAD].astype(v_ref.dtype)


def mla_qkv(x, norm_g, sh, sc, w_dn, cos_t, sin_t, q_a_g, kv_a_g, wq, wk, wv, gq, gkn, gkp, batch, seq, tm=256):
    t, d = x.shape
    per = seq // tm
    hh = MLA_HEADS
    full = lambda shape: pl.BlockSpec(shape, lambda i: (0,) * len(shape))
    return pl.pallas_call(
        _qkv_kernel,
        grid=(t // tm,),
        in_specs=[
            pl.BlockSpec((tm, d), lambda i: (i, 0)),
            full((1, d)),
            pl.BlockSpec((1, 1, d), lambda i: (i // per, 0, 0)),
            pl.BlockSpec((1, 1, d), lambda i: (i // per, 0, 0)),
            full(w_dn.shape),
            pl.BlockSpec((tm, LANES), lambda i: (i, 0)),
            pl.BlockSpec((tm, LANES), lambda i: (i, 0)),
            full((1, Q_LORA)), full((1, KV_LORA)),
            full(wq.shape), full(wk.shape), full(wv.shape),
            full((1, QK_PAD)), full((1, LANES)), full((1, LANES)),
        ],
        out_specs=[
            pl.BlockSpec((1, hh, tm, QK_PAD), lambda i: (i // per, 0, i % per, 0)),
            pl.BlockSpec((1, hh, tm, QK_PAD), lambda i: (i // per, 0, i % per, 0)),
            pl.BlockSpec((1, hh, tm, V_HEAD), lambda i: (i // per, 0, i % per, 0)),
        ],
        out_shape=[
            jax.ShapeDtypeStruct((batch, hh, seq, QK_PAD), BF16),
            jax.ShapeDtypeStruct((batch, hh, seq, QK_PAD), BF16),
            jax.ShapeDtypeStruct((batch, hh, seq, V_HEAD), BF16),
        ],
        compiler_params=_cparams("arbitrary"),
        name="mla_qkv",
    )(x, norm_g.reshape(1, d), sh.reshape(batch, 1, d), sc.reshape(batch, 1, d), w_dn,
      cos_t, sin_t, q_a_g, kv_a_g, wq, wk, wv, gq, gkn, gkp)


def _attn_kernel(q_ref, k_ref, v_ref, o_ref, *, tk):
    q = q_ref[0, 0]
    tq = q.shape[0]
    nk = k_ref.shape[2] // tk

    def body(j, carry):
        m, l, acc = carry
        start = pl.multiple_of(j * tk, tk)
        ks = k_ref[0, 0, pl.ds(start, tk), :]
        vs = v_ref[0, 0, pl.ds(start, tk), :]
        s = lax.dot_general(q, ks, (((1,), (1,)), ((), ())), preferred_element_type=F32)
        m_new = jnp.maximum(m, jnp.max(s, axis=-1, keepdims=True))
        p = jnp.exp2(s - m_new)
        alpha = jnp.exp2(m - m_new)
        l = alpha * l + jnp.sum(p, axis=-1, keepdims=True)
        acc = alpha * acc + jnp.dot(p.astype(BF16), vs, preferred_element_type=F32)
        return m_new, l, acc

    m0 = jnp.full((tq, 1), -jnp.inf, F32)
    l0 = jnp.zeros((tq, 1), F32)
    acc0 = jnp.zeros((tq, V_HEAD), F32)
    _, l, acc = lax.fori_loop(0, nk, body, (m0, l0, acc0), unroll=True)
    o_ref[0] = (acc / l).astype(o_ref.dtype)


def attention(q, k, v, tq=1024, tk=512):
    tq = min(tq, q.shape[2])
    tk = min(tk, q.shape[2])
    b, hh, s, _ = q.shape
    return pl.pallas_call(
        functools.partial(_attn_kernel, tk=tk),
        grid=(b, hh, s // tq),
        in_specs=[
            pl.BlockSpec((1, 1, tq, QK_PAD), lambda bi, hi, qi: (bi, hi, qi, 0)),
            pl.BlockSpec((1, 1, s, QK_PAD), lambda bi, hi, qi: (bi, hi, 0, 0)),
            pl.BlockSpec((1, 1, s, V_HEAD), lambda bi, hi, qi: (bi, hi, 0, 0)),
        ],
        out_specs=pl.BlockSpec((1, tq, V_HEAD), lambda bi, hi, qi: (bi, qi, hi)),
        out_shape=jax.ShapeDtypeStruct((b, s, hh * V_HEAD), BF16),
        compiler_params=_cparams("arbitrary", "arbitrary", "arbitrary"),
        name="attention",
    )(q, k, v)


def _route_rows(logits):
    lane = lax.broadcasted_iota(jnp.int32, logits.shape, 1)
    lane_f = lane.astype(F32)
    neg = jnp.float32(-jnp.inf)
    big = jnp.float32(4 * LANES)
    gmask = (lane >= N_EXPERTS) & (lane < N_EXPERTS + N_GROUPS)
    gl = jnp.where(gmask, logits, neg)
    ge = jnp.exp(gl - jnp.max(gl, axis=-1, keepdims=True))
    gp = ge / jnp.sum(ge, axis=-1, keepdims=True)
    g_w = jnp.max(gp, axis=-1, keepdims=True)
    g_lane = jnp.min(jnp.where(gmask & (gp == g_w), lane_f, big), axis=-1, keepdims=True)
    grp = g_lane - N_EXPERTS
    e_lo = grp * EXPERTS_PER_GROUP
    emask = (lane_f >= e_lo) & (lane_f < e_lo + EXPERTS_PER_GROUP)
    el = jnp.where(emask, logits, neg)
    ee = jnp.exp(el - jnp.max(el, axis=-1, keepdims=True))
    ep = ee / jnp.sum(ee, axis=-1, keepdims=True)
    e1 = jnp.max(ep, axis=-1, keepdims=True)
    i1 = jnp.min(jnp.where(emask & (ep == e1), lane_f, big), axis=-1, keepdims=True)
    rest = emask & (lane_f != i1)
    ep2 = jnp.where(rest, ep, -1.0)
    e2 = jnp.max(ep2, axis=-1, keepdims=True)
    i2 = jnp.min(jnp.where(rest & (ep2 == e2), lane_f, big), axis=-1, keepdims=True)
    tot = e1 + e2
    w4 = jnp.where(lane_f == i1 - e_lo, g_w * (e1 / tot), jnp.where(lane_f == i2 - e_lo, g_w * (e2 / tot), 0.0))
    return grp, w4, lane_f


def _router_kernel(a_ref, wo_ref, x_ref, g1_ref, g_ref, sh_ref, sc_ref, w_ref, b_ref, tri_ref,
                   xn_ref, h_ref, meta_ref, cnt_ref):
    tm, d = x_ref.shape
    slab = tri_ref.shape[0]
    wst = w_ref[...]
    seen = jnp.zeros((1, LANES), F32)
    ranked = []
    for r in range(0, tm, slab):
        rows = slice(r, r + slab)
        xn = x_ref[rows] + g1_ref[0] * jnp.dot(a_ref[rows], wo_ref[...], preferred_element_type=F32)
        xn_ref[rows] = xn
        h = _norm_mod(xn, g_ref[...], sh_ref[0], sc_ref[0])
        hb = h.astype(BF16)
        rem = h - hb.astype(F32)
        hm = rem.astype(BF16)
        hl = (rem - hm.astype(F32)).astype(BF16)
        part = (jnp.dot(hb, wst, preferred_element_type=F32) + jnp.dot(hm, wst, preferred_element_type=F32)
                + jnp.dot(hl, wst, preferred_element_type=F32))
        logits = (part + pltpu.roll(part, LANES - ROUTER_LANES, 1)
                  + pltpu.roll(part, LANES - 2 * ROUTER_LANES, 1)) + b_ref[...]
        grp, w4, lane_f = _route_rows(logits)
        hi = w4.astype(BF16).astype(F32)
        mid = (w4 - hi).astype(BF16).astype(F32)
        lo = (w4 - hi - mid).astype(BF16).astype(F32)
        ext = hi + pltpu.roll(mid, EXPERTS_PER_GROUP, 1) + pltpu.roll(lo, 2 * EXPERTS_PER_GROUP, 1)
        h_ref[rows, :d] = hb
        h_ref[rows, d:] = ext.astype(h_ref.dtype)
        onehot = lane_f == grp
        ohf = jnp.where(onehot, 1.0, 0.0)
        rank = jnp.dot(tri_ref[...], ohf.astype(BF16), preferred_element_type=F32) + seen
        seen = seen + jnp.sum(ohf, axis=0, keepdims=True)
        ranked.append((rows, onehot, rank))
    counts = jnp.broadcast_to(seen, (8, LANES))
    starts = pltpu.roll(counts, 1, 1) + pltpu.roll(counts, 2, 1) + pltpu.roll(counts, 3, 1)
    for rows, onehot, rank in ranked:
        pos = jnp.sum(jnp.where(onehot, rank - 1.0 + starts[0:1], 0.0), axis=-1, keepdims=True)
        meta_ref[rows] = jnp.broadcast_to(pos, (rows.stop - rows.start, LANES))
    cnt_ref[0] = counts


def moe_router(a, w_o, x, gate, g, sh, sc, w_r, b_r, seq, tm):
    t, d = x.shape
    k = a.shape[1]
    per = seq // tm
    nb = sh.shape[0]
    slab = min(MOE_SLAB, tm)
    tri = jnp.tril(jnp.ones((slab, slab), BF16))
    return pl.pallas_call(
        _router_kernel,
        grid=(t // tm,),
        in_specs=[
            pl.BlockSpec((tm, k), lambda i: (i, 0)),
            pl.BlockSpec((k, d), lambda i: (0, 0)),
            pl.BlockSpec((tm, d), lambda i: (i, 0)),
            pl.BlockSpec((1, 1, d), lambda i: (i // per, 0, 0)),
            pl.BlockSpec((1, d), lambda i: (0, 0)),
            pl.BlockSpec((1, 1, d), lambda i: (i // per, 0, 0)),
            pl.BlockSpec((1, 1, d), lambda i: (i // per, 0, 0)),
            pl.BlockSpec((d, LANES), lambda i: (0, 0)),
            pl.BlockSpec((1, LANES), lambda i: (0, 0)),
            pl.BlockSpec((slab, slab), lambda i: (0, 0)),
        ],
        out_specs=[
            pl.BlockSpec((tm, d), lambda i: (i, 0)),
            pl.BlockSpec((tm, d + LANES), lambda i: (i, 0)),
            pl.BlockSpec((tm, LANES), lambda i: (i, 0)),
            pl.BlockSpec((1, 8, LANES), lambda i: (i, 0, 0)),
        ],
        out_shape=[
            jax.ShapeDtypeStruct((t, d), F32),
            jax.ShapeDtypeStruct((t, d + LANES), BF16),
            jax.ShapeDtypeStruct((t, LANES), F32),
            jax.ShapeDtypeStruct((t // tm, 8, LANES), F32),
        ],
        compiler_params=_cparams("arbitrary"),
        name="moe_router",
    )(a, w_o, x, gate.reshape(nb, 1, d), g.reshape(1, d), sh.reshape(nb, 1, d), sc.reshape(nb, 1, d), w_r, b_r, tri)


def _moe_kernel(st_ref, en_ref, h_ref, meta_ref, posr_ref, wg_ref, wu_ref, wd_ref, x_ref, gate_ref, o_ref,
                hs_ref, y_ref):
    i = pl.program_id(0)
    g = pl.program_id(1)
    tm, d = x_ref.shape
    slab = min(MOE_SLAB, tm)

    @pl.when(g == 0)
    def _():
        pos_row = posr_ref[0]
        for r in range(0, tm, slab):
            rows = lax.broadcasted_iota(jnp.int32, (slab, tm), 0).astype(F32) + float(r)
            perm = jnp.where(rows == pos_row, 1.0, 0.0).astype(BF16)
            hs_ref[r:r + slab] = jnp.dot(perm, h_ref[...], preferred_element_type=F32).astype(BF16)
        y_ref[...] = jnp.zeros_like(y_ref)

    start = st_ref[i * N_GROUPS + g]
    end = en_ref[i * N_GROUPS + g]
    shift = MOE_CHUNK.bit_length() - 1
    c_lo = lax.shift_right_logical(start, shift)
    c_hi = jnp.where(end > start, lax.shift_right_logical(end + (MOE_CHUNK - 1), shift), c_lo)

    def chunk(c, carry):
        r0 = pl.multiple_of(c * MOE_CHUNK, MOE_CHUNK)
        hx = hs_ref[pl.ds(r0, MOE_CHUNK), :]
        hs = hx[:, :d]
        ext = hx[:, d:].astype(F32)
        w4 = (ext + pltpu.roll(ext, LANES - EXPERTS_PER_GROUP, 1)
              + pltpu.roll(ext, LANES - 2 * EXPERTS_PER_GROUP, 1))
        rowid = r0 + lax.broadcasted_iota(jnp.int32, w4.shape, 0)
        w4 = jnp.where((rowid >= start) & (rowid < end), w4, 0.0)
        lane = lax.broadcasted_iota(jnp.int32, w4.shape, 1)
        acc = jnp.zeros((MOE_CHUNK, d), F32)
        for j in range(EXPERTS_PER_GROUP):
            a = jnp.dot(hs, wg_ref[j], preferred_element_type=F32)
            u = jnp.dot(hs, wu_ref[j], preferred_element_type=F32)
            wj = jnp.sum(jnp.where(lane == j, w4, 0.0), axis=-1, keepdims=True)
            act = (a * jax.nn.sigmoid(a)) * u * wj
            acc = acc + jnp.dot(act.astype(BF16), wd_ref[j], preferred_element_type=F32)
        y_ref[pl.ds(r0, MOE_CHUNK), :] += acc
        return carry

    lax.fori_loop(c_lo, c_hi, chunk, 0)

    @pl.when(g == N_GROUPS - 1)
    def _():
        yb = y_ref[...].astype(BF16)
        for r in range(0, tm, slab):
            pos_col = meta_ref[r:r + slab, 0:1]
            cols = lax.broadcasted_iota(jnp.int32, (slab, tm), 1).astype(F32)
            perm_t = jnp.where(cols == pos_col, 1.0, 0.0).astype(BF16)
            y = jnp.dot(perm_t, yb, preferred_element_type=F32)
            o_ref[r:r + slab] = x_ref[r:r + slab] + gate_ref[0] * y


def moe_experts(h_ext, meta, counts, wg, wu, wd, x, gate, seq, tm):
    t, d = x.shape
    f = wg.shape[2]
    per = seq // tm
    nb = gate.shape[0]
    ntiles = t // tm
    cnt = counts[:, 0, :N_GROUPS].astype(jnp.int32)
    ends = jnp.cumsum(cnt, axis=1)
    starts = ends - cnt
    pos_row = meta[:, 0].reshape(ntiles, 1, tm)
    epg = EXPERTS_PER_GROUP
    grid_spec = pltpu.PrefetchScalarGridSpec(
        num_scalar_prefetch=2,
        grid=(ntiles, N_GROUPS),
        in_specs=[
            pl.BlockSpec((tm, d + LANES), lambda i, g, st, en: (i, 0)),
            pl.BlockSpec((tm, LANES), lambda i, g, st, en: (i, 0)),
            pl.BlockSpec((1, 1, tm), lambda i, g, st, en: (i, 0, 0)),
            pl.BlockSpec((epg, d, f), lambda i, g, st, en: (g, 0, 0)),
            pl.BlockSpec((epg, d, f), lambda i, g, st, en: (g, 0, 0)),
            pl.BlockSpec((epg, f, d), lambda i, g, st, en: (g, 0, 0)),
            pl.BlockSpec((tm, d), lambda i, g, st, en: (i, 0)),
            pl.BlockSpec((1, 1, d), lambda i, g, st, en: (i // per, 0, 0)),
        ],
        out_specs=pl.BlockSpec((tm, d), lambda i, g, st, en: (i, 0)),
        scratch_shapes=[pltpu.VMEM((tm, d + LANES), BF16), pltpu.VMEM((tm, d), F32)],
    )
    return pl.pallas_call(
        _moe_kernel,
        grid_spec=grid_spec,
        out_shape=jax.ShapeDtypeStruct((t, d), F32),
        compiler_params=_cparams("arbitrary", "arbitrary"),
        name="moe_experts",
    )(starts.reshape(-1), ends.reshape(-1), h_ext, meta, pos_row, wg, wu, wd, x, gate.reshape(nb, 1, d))


def _filter_kernel(z_ref, w1_ref, b1_ref, f1_ref, w2_ref, b2_ref, f2_ref, w3_ref, b3_ref, f3_ref,
                   w4_ref, delta_ref, o_ref, hdn_ref):
    dot = functools.partial(jnp.dot, precision=HIGHEST, preferred_element_type=F32)

    @pl.when(pl.program_id(0) == 0)
    def _():
        hdn = jnp.sin(f1_ref[...] * (dot(z_ref[...], w1_ref[...]) + b1_ref[...]))
        hdn = jnp.sin(f2_ref[...] * (dot(hdn, w2_ref[...]) + b2_ref[...]))
        hdn_ref[...] = jnp.sin(f3_ref[...] * (dot(hdn, w3_ref[...]) + b3_ref[...]))

    h = dot(hdn_ref[...], w4_ref[...])
    t = z_ref[:, 0:1]
    h = h * jnp.exp(-t * delta_ref[...])
    o_ref[...] = (h / (jnp.sum(jnp.abs(h), axis=0, keepdims=True) + EPS)).astype(o_ref.dtype)


def hyena_filter(z, w1, b1, f1, w2, b2, f2, w3, b3, f3, w4, deltas, tn=256):
    length, emb = z.shape
    width = w2.shape[0]
    n = w4.shape[1]
    full = lambda shape: pl.BlockSpec(shape, lambda j: (0,) * len(shape))
    vec = lambda v: v.reshape(1, -1)
    return pl.pallas_call(
        _filter_kernel,
        grid=(n // tn,),
        in_specs=[
            full((length, emb)),
            full((emb, width)), full((1, width)), full((1, width)),
            full((width, width)), full((1, width)), full((1, width)),
            full((width, width)), full((1, width)), full((1, width)),
            pl.BlockSpec((width, tn), lambda j: (0, j)),
            pl.BlockSpec((1, tn), lambda j: (0, j)),
        ],
        out_specs=pl.BlockSpec((length, tn), lambda j: (0, j)),
        out_shape=jax.ShapeDtypeStruct((length, n), BF16),
        scratch_shapes=[pltpu.VMEM((length, width), F32)],
        compiler_params=_cparams("arbitrary"),
        name="hyena_filter",
    )(z, w1, vec(b1), vec(f1), w2, vec(b2), vec(f2), w3, vec(b3), vec(f3), w4, vec(deltas))


def _bmm_kernel(a_ref, x_ref, o_ref):
    o_ref[0] = jnp.dot(a_ref[...], x_ref[0], preferred_element_type=F32).astype(o_ref.dtype)


def block_dft(a, x, col0, ncols, out_dtype, tn=512, name="block_dft"):
    m, k = a.shape
    g = x.shape[0]
    c0 = col0 // tn
    return pl.pallas_call(
        _bmm_kernel,
        grid=(g, ncols // tn),
        in_specs=[
            pl.BlockSpec((m, k), lambda gi, j: (0, 0)),
            pl.BlockSpec((1, k, tn), lambda gi, j: (gi, 0, c0 + j)),
        ],
        out_specs=pl.BlockSpec((1, m, tn), lambda gi, j: (gi, 0, j)),
        out_shape=jax.ShapeDtypeStruct((g, m, ncols), out_dtype),
        compiler_params=_cparams("arbitrary", "arbitrary"),
        name=name,
    )(a, x)


CONV_HALO = 16


def _hyena_conv_kernel(fwd_ref, inv_ref, u_ref, g_ref, cw_ref, cb_ref, p_ref, q_ref, r_ref, bias_ref,
                       o_ref, uf_ref, us_ref, *, nblk, conv_u):
    s = u_ref.shape[1]
    tb = s // nblk

    def short_conv_block(ref, i, which):
        lo = max(i * tb - CONV_HALO, 0)
        hi = min((i + 1) * tb + CONV_HALO, s)
        x = ref[0, lo:hi].astype(F32)
        n = hi - lo
        prev = pltpu.roll(x, 1, 0)
        nxt = pltpu.roll(x, n - 1, 0)
        if lo == 0:
            prev = jnp.where(lax.broadcasted_iota(jnp.int32, x.shape, 0) == 0, 0.0, prev)
        if hi == s:
            nxt = jnp.where(lax.broadcasted_iota(jnp.int32, x.shape, 0) == n - 1, 0.0, nxt)
        w = cw_ref[which]
        y = prev * w[0:1] + x * w[1:2] + nxt * w[2:3] + cb_ref[which]
        off = i * tb - lo
        return y[off:off + tb]

    for j in range(nblk):
        rows = slice(j * tb, (j + 1) * tb)
        ub = short_conv_block(u_ref, j, 0).astype(BF16) if conv_u else u_ref[0, rows]
        us_ref[rows] = ub
        uf_ref[j] = jnp.dot(fwd_ref[...], ub, preferred_element_type=F32).astype(uf_ref.dtype)

    for i in range(nblk):
        re_acc = None
        im_acc = None
        for j in range(nblk):
            lag = i - j + nblk - 1
            re = uf_ref[j, :tb]
            im = uf_ref[j, tb:]
            p = p_ref[lag]
            q = q_ref[lag]
            r = r_ref[lag]
            tre = re * p - im * q
            tim = re * q + im * r
            re_acc = tre if re_acc is None else re_acc + tre
            im_acc = tim if im_acc is None else im_acc + tim
        yf = jnp.concatenate([re_acc, im_acc], axis=0)
        y = jnp.dot(inv_ref[...], yf, preferred_element_type=F32)
        rows = slice(i * tb, (i + 1) * tb)
        gate = short_conv_block(g_ref, i, 1)
        o_ref[0, rows] = (gate * (y + us_ref[rows].astype(F32) * bias_ref[...])).astype(o_ref.dtype)


def hyena_conv(fwd, inv, u_arr, u_col0, conv_u, g_arr, g_col0, cw, cb, p, q, r, p_col0, bias, nblk, name, tn=256):
    nb, s, _ = u_arr.shape
    d = bias.shape[0]
    tb = s // nblk
    nlag = p.shape[0]
    uc0, gc0, pc0 = u_col0 // tn, g_col0 // tn, p_col0 // tn
    once = pl.Buffered(1)
    hspec = pl.BlockSpec((nlag, tb, tn), lambda ci, bi: (0, 0, pc0 + ci), pipeline_mode=once)
    return pl.pallas_call(
        functools.partial(_hyena_conv_kernel, nblk=nblk, conv_u=conv_u),
        grid=(d // tn, nb),
        in_specs=[
            pl.BlockSpec(fwd.shape, lambda ci, bi: (0, 0), pipeline_mode=once),
            pl.BlockSpec(inv.shape, lambda ci, bi: (0, 0), pipeline_mode=once),
            pl.BlockSpec((1, s, tn), lambda ci, bi: (bi, 0, uc0 + ci)),
            pl.BlockSpec((1, s, tn), lambda ci, bi: (bi, 0, gc0 + ci)),
            pl.BlockSpec((2, 3, tn), lambda ci, bi: (0, 0, ci)),
            pl.BlockSpec((2, 1, tn), lambda ci, bi: (0, 0, ci)),
            hspec, hspec, hspec,
            pl.BlockSpec((1, tn), lambda ci, bi: (0, ci)),
        ],
        out_specs=pl.BlockSpec((1, s, tn), lambda ci, bi: (bi, 0, ci)),
        out_shape=jax.ShapeDtypeStruct((nb, s, d), BF16),
        scratch_shapes=[pltpu.VMEM((nblk, 2 * tb, tn), BF16), pltpu.VMEM((s, tn), BF16)],
        compiler_params=_cparams("arbitrary", "arbitrary"),
        name=name,
    )(fwd, inv, u_arr, g_arr, cw, cb, p, q, r, bias.reshape(1, d))


def _dft_matrices(tb):
    n = 2 * tb
    k = np.arange(tb, dtype=np.float64)[:, None]
    t = np.arange(tb, dtype=np.float64)[None, :]
    ang = 2.0 * np.pi * k * t / n
    fre = np.cos(ang)
    fim = -np.sin(ang)
    fim[0, :] = np.cos(np.pi * t[0])
    fwd = np.concatenate([fre, fim], axis=0)
    wre = np.full((tb, 1), 2.0)
    wre[0, 0] = 1.0
    ire = wre * np.cos(ang) / n
    iim = -2.0 * np.sin(ang) / n
    iim[0, :] = np.cos(np.pi * t[0]) / n
    inv = np.concatenate([ire, iim], axis=0).T
    sign = np.where(np.arange(tb) % 2 == 0, 1.0, -1.0)
    return jnp.asarray(fwd, BF16), jnp.asarray(inv, BF16), jnp.asarray(sign, F32)


def _fspec_kernel(ef_ref, eb_ref, tf_ref, tb_ref, s_ref, p_ref, q_ref, r_ref):
    nblk = ef_ref.shape[0]
    sgn = s_ref[...]
    shape = ef_ref.shape[2:]
    row0 = (lax.broadcasted_iota(jnp.int32, shape, 0) == 0) & (pl.program_id(1) == 0)

    def block(m):
        if m >= 0:
            return ef_ref[m, 0].astype(F32), ef_ref[m, 1].astype(F32), tf_ref[m]
        mu = -m
        tap = tb_ref[mu - 1]
        xre = eb_ref[mu - 1, 0].astype(F32) - tap
        xim = eb_ref[mu - 1, 1].astype(F32) - jnp.where(row0, tap, 0.0)
        b0 = tb_ref[mu] if mu < nblk else jnp.zeros_like(tap)
        return b0 + sgn * xre, jnp.where(row0, b0 + xim, -sgn * xim), b0

    prev = block(-nblk)
    for m in range(-(nblk - 1), nblk):
        cur = block(m)
        hre = cur[0] + sgn * (prev[0] - prev[2])
        him = cur[1] + sgn * (prev[1] - jnp.where(row0, prev[2], 0.0))
        lag = m + nblk - 1
        p_ref[lag] = hre.astype(p_ref.dtype)
        q_ref[lag] = jnp.where(row0, 0.0, him).astype(q_ref.dtype)
        r_ref[lag] = jnp.where(row0, him, hre).astype(r_ref.dtype)
        prev = cur


def filter_spectra(filt, fwd, sign, nblk, tb, tf=256, tn=256):
    d = D_MODEL
    tf = min(tf, tb)
    ncol = filt.shape[1]
    fb = filt.reshape(nblk, tb, ncol)
    e = block_dft(fwd, fb, 0, ncol, BF16, name="filter_dft").reshape(nblk, 2, tb, ncol)
    taps = fb[:, 0:1, :].astype(F32)
    nlag = 2 * nblk - 1
    per = d // tn
    espec = lambda direction: pl.BlockSpec(
        (nblk, 2, tf, tn), lambda o, fi, ci: (0, 0, fi, (2 * o + direction) * per + ci))
    tspec = lambda direction: pl.BlockSpec(
        (nblk, 1, tn), lambda o, fi, ci: (0, 0, (2 * o + direction) * per + ci))
    ospec = pl.BlockSpec((nlag, tf, tn), lambda o, fi, ci: (0, fi, o * per + ci))
    oshape = jax.ShapeDtypeStruct((nlag, tb, HYENA_ORDER * d), BF16)
    return pl.pallas_call(
        _fspec_kernel,
        grid=(HYENA_ORDER, tb // tf, per),
        in_specs=[espec(0), espec(1), tspec(0), tspec(1), pl.BlockSpec((tf, 1), lambda o, fi, ci: (fi, 0))],
        out_specs=[ospec, ospec, ospec],
        out_shape=[oshape, oshape, oshape],
        compiler_params=_cparams("arbitrary", "arbitrary", "arbitrary"),
        name="filter_spectra",
    )(e, e, taps, taps, sign.reshape(tb, 1))


def kernel(x, c, positions, ada_w, ada_b, norm_mix_g, norm_ffn_g, mla_w_down, mla_q_a_g, mla_kv_a_g, mla_w_uq, mla_w_ukv, mla_q_norm_g, mla_k_norm_g, mla_w_o, hy_w_in, hy_b_in, hy_conv_w, hy_conv_b, hy_f_w1, hy_f_b1, hy_f_freq1, hy_f_w2, hy_f_b2, hy_f_freq2, hy_f_w3, hy_f_b3, hy_f_freq3, hy_f_w4, hy_filt_bias, hy_w_out, moe_wg, moe_bg, moe_we, moe_be, moe_w_gate, moe_w_up, moe_w_down):
    batch, seq, d = x.shape
    t = batch * seq
    hh = MLA_HEADS
    xf = x.reshape(t, d)

    mod = adaln(c, ada_w, ada_b)

    def mods(i):
        return [mod[i, :, j * d:(j + 1) * d] for j in range(6)]

    def moe_layer(a, w_o, xin, g1, i, sh2, sc2, g2):
        w_r = jnp.concatenate([moe_we[i], moe_wg[i], jnp.zeros((d, ROUTER_LANES - N_EXPERTS - N_GROUPS), F32)], axis=1)
        w_hi = w_r.astype(BF16)
        w_rem = w_r - w_hi.astype(F32)
        w_mid = w_rem.astype(BF16)
        w_lo = (w_rem - w_mid.astype(F32)).astype(BF16)
        w_r = jnp.concatenate([w_hi, w_mid, w_lo, jnp.zeros((d, LANES - 3 * ROUTER_LANES), BF16)], axis=1)
        b_r = jnp.concatenate([moe_be[i], moe_bg[i], jnp.zeros((LANES - N_EXPERTS - N_GROUPS,), F32)]).reshape(1, LANES)
        tm = min(MOE_TILE, seq)
        xn, h_ext, meta, counts = moe_router(a, w_o.astype(BF16), xin, g1, norm_ffn_g[i], sh2, sc2, w_r, b_r, seq, tm)
        return moe_experts(h_ext, meta, counts, moe_w_gate[i].astype(BF16), moe_w_up[i].astype(BF16),
                           moe_w_down[i].astype(BF16), xn, g2, seq, tm)

    sh1, sc1, g1, sh2, sc2, g2 = mods(0)
    lat_w = mla_w_down[0]
    w_dn = jnp.concatenate([lat_w, lat_w[:, Q_LORA + KV_LORA:]], axis=1).astype(BF16)

    inv_freq = 1.0 / (ROPE_THETA ** (jnp.arange(0, QK_ROPE, 2, dtype=F32) / QK_ROPE))
    ang = positions.astype(F32)[..., None] * inv_freq
    pad = jnp.zeros((batch, seq, LANES - QK_ROPE), F32)
    cos_t = jnp.concatenate([jnp.cos(ang), jnp.cos(ang), pad], axis=-1).reshape(t, LANES)
    sin_t = jnp.concatenate([-jnp.sin(ang), jnp.sin(ang), pad], axis=-1).reshape(t, LANES)

    wq = mla_w_uq[0].reshape(Q_LORA, hh, QK_HEAD)
    wq = jnp.concatenate([wq, wq[:, :, QK_NOPE:]], axis=-1)
    wq = wq.reshape(Q_LORA, hh * QK_PAD).astype(BF16)
    wkv = mla_w_ukv[0].reshape(KV_LORA, hh, QK_NOPE + V_HEAD)
    wk = wkv[:, :, :QK_NOPE].reshape(KV_LORA, hh * QK_NOPE).astype(BF16)
    wv = wkv[:, :, QK_NOPE:].reshape(KV_LORA, hh * V_HEAD).astype(BF16)
    gq = jnp.concatenate([mla_q_norm_g[0], mla_q_norm_g[0][QK_NOPE:]]).reshape(1, QK_PAD)
    gkn = mla_k_norm_g[0][:QK_NOPE].reshape(1, LANES)
    gkp = jnp.concatenate([mla_k_norm_g[0][QK_NOPE:], mla_k_norm_g[0][QK_NOPE:]]).reshape(1, LANES)
    q, k, v = mla_qkv(xf, norm_mix_g[0], sh1, sc1, w_dn, cos_t, sin_t, mla_q_a_g[0].reshape(1, Q_LORA), mla_kv_a_g[0].reshape(1, KV_LORA),
                      wq, wk, wv, gq, gkn, gkp, batch, seq)
    o = attention(q, k, v)
    xf = moe_layer(o.reshape(t, hh * V_HEAD), mla_w_o[0], xf, g1, 0, sh2, sc2, g2)

    sh1, sc1, g1, sh2, sc2, g2 = mods(1)
    nblk = CONV_BLOCKS
    tb = seq // nblk
    fwd, inv, sign = _dft_matrices(tb)

    tt = jnp.linspace(0.0, 1.0, seq, dtype=F32)[:, None]
    wfreq = 2.0 * math.pi * jnp.arange(seq, dtype=F32)[:, None] / seq
    fr = jnp.linspace(1e-4, FILTER_BANDS - 1, FILTER_BANDS, dtype=F32)[None, :]
    z = jnp.concatenate([tt, jnp.cos(fr * wfreq), -jnp.sin(fr * wfreq)], axis=-1)
    deltas = jnp.abs(jnp.linspace(math.log(FAST_DECAY) / DECAY_TARGET, math.log(SLOW_DECAY) / DECAY_TARGET, d, dtype=F32))
    filt = hyena_filter(z, hy_f_w1[0], hy_f_b1[0], hy_f_freq1[0], hy_f_w2[0], hy_f_b2[0], hy_f_freq2[0],
                        hy_f_w3[0], hy_f_b3[0], hy_f_freq3[0], hy_f_w4[0], jnp.tile(deltas, HYENA_ORDER * 2))
    p, qc, r = filter_spectra(filt, fwd, sign, nblk, tb)

    u = norm_mod_matmul(xf, norm_mix_g[1], sh1, sc1, hy_w_in[0].astype(BF16), hy_b_in[0], seq, BF16, name="hy_in")
    u3 = u.reshape(batch, seq, 3 * d)
    cw3 = hy_conv_w[0].reshape(3, 3, d)
    cb3 = hy_conv_b[0].reshape(3, 1, d)

    zsrc, zcol = u3, 2 * d
    for order in range(HYENA_ORDER):
        cw = jnp.stack([cw3[:, 2], cw3[:, order]], axis=0)
        cb = jnp.stack([cb3[2], cb3[order]], axis=0)
        zsrc = hyena_conv(fwd, inv, zsrc, zcol, order == 0, u3, order * d, cw, cb, p, qc, r, order * d,
                          hy_filt_bias[0, order], nblk, name=f"hyena_conv{order}")
        zcol = 0
    xf = moe_layer(zsrc.reshape(t, d), hy_w_out[0], xf, g1, 1, sh2, sc2, g2)
    return xf.reshape(batch, seq, d)
```

```python
import functools
import math

import jax
import jax.numpy as jnp
import numpy as np
from jax import lax
from jax.experimental import pallas as pl
from jax.experimental.pallas import tpu as pltpu

F32 = jnp.float32
BF16 = jnp.bfloat16
HIGHEST = lax.Precision.HIGHEST

D_MODEL = 1024
MLA_HEADS = 8
Q_LORA = 256
KV_LORA = 128
QK_NOPE = 128
QK_ROPE = 64
QK_HEAD = QK_NOPE + QK_ROPE
V_HEAD = 128
ROPE_HALF = QK_ROPE // 2
ROPE_THETA = 10000.0
HYENA_ORDER = 2
FILTER_EMB = 33
FILTER_BANDS = (FILTER_EMB - 1) // 2
FAST_DECAY = 0.3
SLOW_DECAY = 1.5
DECAY_TARGET = 1e-2
N_GROUPS = 4
EXPERTS_PER_GROUP = 4
N_EXPERTS = N_GROUPS * EXPERTS_PER_GROUP
D_EXPERT = 256
EPS = 1e-6

LANES = 128
QK_PAD = 2 * LANES
CONV_BLOCKS = 8
ROUTER_LANES = 32
MOE_TILE = 1024
MOE_CHUNK = 128
MOE_SLAB = 256
QKV_SLAB = 256
VMEM_LIMIT = 56 * 1024 * 1024


def _cparams(*sem):
    return pltpu.CompilerParams(dimension_semantics=sem, vmem_limit_bytes=VMEM_LIMIT)


def _adaln_kernel(c_ref, w_ref, b_ref, o_ref):
    c = c_ref[...]
    ca = c * jax.nn.sigmoid(c)
    o_ref[0] = jnp.dot(ca, w_ref[0], precision=HIGHEST, preferred_element_type=F32) + b_ref[0]


def adaln(c, ada_w, ada_b, tn=1536):
    depth, d, n = ada_w.shape
    b = c.shape[0]
    return pl.pallas_call(
        _adaln_kernel,
        grid=(depth, n // tn),
        in_specs=[
            pl.BlockSpec((b, d), lambda i, j: (0, 0)),
            pl.BlockSpec((1, d, tn), lambda i, j: (i, 0, j)),
            pl.BlockSpec((1, 1, tn), lambda i, j: (i, 0, j)),
        ],
        out_specs=pl.BlockSpec((1, b, tn), lambda i, j: (i, 0, j)),
        out_shape=jax.ShapeDtypeStruct((depth, b, n), F32),
        compiler_params=_cparams("arbitrary", "arbitrary"),
        name="adaln",
    )(c, ada_w, ada_b.reshape(depth, 1, n))


def _norm_mod(x, g, sh, sc):
    ms = jnp.mean(x * x, axis=-1, keepdims=True)
    y = x * lax.rsqrt(ms + EPS) * g
    return y * (1.0 + sc) + sh


def _nmm_kernel(x_ref, g_ref, sh_ref, sc_ref, w_ref, b_ref, o_ref):
    h = _norm_mod(x_ref[...], g_ref[...], sh_ref[0], sc_ref[0])
    o = jnp.dot(h.astype(BF16), w_ref[...], preferred_element_type=F32) + b_ref[...]
    o_ref[...] = o.astype(o_ref.dtype)


def norm_mod_matmul(x, g, sh, sc, w, b, seq, out_dtype, tm=512, name="nmm"):
    t, d = x.shape
    n = w.shape[1]
    per = seq // tm
    nb = sh.shape[0]
    return pl.pallas_call(
        _nmm_kernel,
        grid=(t // tm,),
        in_specs=[
            pl.BlockSpec((tm, d), lambda i: (i, 0)),
            pl.BlockSpec((1, d), lambda i: (0, 0)),
            pl.BlockSpec((1, 1, d), lambda i: (i // per, 0, 0)),
            pl.BlockSpec((1, 1, d), lambda i: (i // per, 0, 0)),
            pl.BlockSpec((d, n), lambda i: (0, 0)),
            pl.BlockSpec((1, n), lambda i: (0, 0)),
        ],
        out_specs=pl.BlockSpec((tm, n), lambda i: (i, 0)),
        out_shape=jax.ShapeDtypeStruct((t, n), out_dtype),
        compiler_params=_cparams("arbitrary"),
        name=name,
    )(x, g.reshape(1, d), sh.reshape(nb, 1, d), sc.reshape(nb, 1, d), w, b.reshape(1, n))


def _rms(x, n):
    return x * lax.rsqrt(jnp.sum(x * x, axis=-1, keepdims=True) * (1.0 / n) + EPS)


def _qkv_kernel(x_ref, ng_ref, sh_ref, sc_ref, wdn_ref, cos_ref, sin_ref, qag_ref, kvag_ref, wq_ref, wk_ref, wv_ref,
                gq_ref, gkn_ref, gkp_ref, q_ref, k_ref, v_ref):
    tm = x_ref.shape[0]
    slab = min(QKV_SLAB, tm)
    gq = gq_ref[...]
    scale = QK_HEAD ** -0.5 * math.log2(math.e)
    for r in range(0, tm, slab):
        rows = slice(r, r + slab)
        h = _norm_mod(x_ref[rows], ng_ref[...], sh_ref[0], sc_ref[0])
        lat = jnp.dot(h.astype(BF16), wdn_ref[...], preferred_element_type=F32)
        cq = lat[:, :Q_LORA]
        ckv = lat[:, Q_LORA:Q_LORA + KV_LORA]
        kpe = lat[:, Q_LORA + KV_LORA:]
        cqn = (_rms(cq, Q_LORA) * qag_ref[...]).astype(BF16)
        ckvn = (_rms(ckv, KV_LORA) * kvag_ref[...]).astype(BF16)
        qall = jnp.dot(cqn, wq_ref[...], preferred_element_type=F32)
        kall = jnp.dot(ckvn, wk_ref[...], preferred_element_type=F32)
        vall = jnp.dot(ckvn, wv_ref[...], preferred_element_type=F32)
        cos_t = cos_ref[rows]
        sin_t = sin_ref[rows]

        def rope(pe):
            return pe * cos_t + pltpu.roll(pe, ROPE_HALF, 1) * sin_t

        kp = rope(_rms(kpe, 2 * QK_ROPE) * gkp_ref[...])
        for h in range(MLA_HEADS):
            qh = qall[:, h * QK_PAD:(h + 1) * QK_PAD]
            qn = _rms(qh[:, :QK_NOPE], QK_NOPE) * gq[:, :QK_NOPE]
            qp = rope(_rms(qh[:, QK_NOPE:], 2 * QK_ROPE) * gq[:, QK_NOPE:])
            q_ref[0, h, rows] = (jnp.concatenate([qn, qp], axis=-1) * scale).astype(q_ref.dtype)
            kn = _rms(kall[:, h * QK_NOPE:(h + 1) * QK_NOPE], QK_NOPE) * gkn_ref[...]
            k_ref[0, h, rows] = jnp.concatenate([kn, kp], axis=-1).astype(k_ref.dtype)
            v_ref[0, h, rows] = vall[:, h * V_HEAD:(h + 1) * V_HEAD].astype(v_ref.dtype)


def mla_qkv(x, norm_g, sh, sc, w_dn, cos_t, sin_t, q_a_g, kv_a_g, wq, wk, wv, gq, gkn, gkp, batch, seq, tm=512):
    t, d = x.shape
    per = seq // tm
    hh = MLA_HEADS
    full = lambda shape: pl.BlockSpec(shape, lambda i: (0,) * len(shape))
    return pl.pallas_call(
        _qkv_kernel,
        grid=(t // tm,),
        in_specs=[
            pl.BlockSpec((tm, d), lambda i: (i, 0)),
            full((1, d)),
            pl.BlockSpec((1, 1, d), lambda i: (i // per, 0, 0)),
            pl.BlockSpec((1, 1, d), lambda i: (i // per, 0, 0)),
            full(w_dn.shape),
            pl.BlockSpec((tm, LANES), lambda i: (i, 0)),
            pl.BlockSpec((tm, LANES), lambda i: (i, 0)),
            full((1, Q_LORA)), full((1, KV_LORA)),
            full(wq.shape), full(wk.shape), full(wv.shape),
            full((1, QK_PAD)), full((1, LANES)), full((1, LANES)),
        ],
        out_specs=[
            pl.BlockSpec((1, hh, tm, QK_PAD), lambda i: (i // per, 0, i % per, 0)),
            pl.BlockSpec((1, hh, tm, QK_PAD), lambda i: (i // per, 0, i % per, 0)),
            pl.BlockSpec((1, hh, tm, V_HEAD), lambda i: (i // per, 0, i % per, 0)),
        ],
        out_shape=[
            jax.ShapeDtypeStruct((batch, hh, seq, QK_PAD), BF16),
            jax.ShapeDtypeStruct((batch, hh, seq, QK_PAD), BF16),
            jax.ShapeDtypeStruct((batch, hh, seq, V_HEAD), BF16),
        ],
        compiler_params=_cparams("arbitrary"),
        name="mla_qkv",
    )(x, norm_g.reshape(1, d), sh.reshape(batch, 1, d), sc.reshape(batch, 1, d), w_dn,
      cos_t, sin_t, q_a_g, kv_a_g, wq, wk, wv, gq, gkn, gkp)


def _attn_kernel(q_ref, k_ref, v_ref, o_ref, *, tk):
    q = q_ref[0, 0]
    tq = q.shape[0]
    nk = k_ref.shape[2] // tk

    def body(j, carry):
        m, l, acc = carry
        start = pl.multiple_of(j * tk, tk)
        ks = k_ref[0, 0, pl.ds(start, tk), :]
        vs = v_ref[0, 0, pl.ds(start, tk), :]
        s = lax.dot_general(q, ks, (((1,), (1,)), ((), ())), preferred_element_type=F32)
        m_new = jnp.maximum(m, jnp.max(s, axis=-1, keepdims=True))
        p = jnp.exp2(s - m_new)
        alpha = jnp.exp2(m - m_new)
        l = alpha * l + jnp.sum(p, axis=-1, keepdims=True)
        acc = alpha * acc + jnp.dot(p.astype(BF16), vs, preferred_element_type=F32)
        return m_new, l, acc

    m0 = jnp.full((tq, 1), -jnp.inf, F32)
    l0 = jnp.zeros((tq, 1), F32)
    acc0 = jnp.zeros((tq, V_HEAD), F32)
    _, l, acc = lax.fori_loop(0, nk, body, (m0, l0, acc0), unroll=True)
    o_ref[0] = (acc / l).astype(o_ref.dtype)


def attention(q, k, v, tq=2048, tk=256):
    tq = min(tq, q.shape[2])
    tk = min(tk, q.shape[2])
    b, hh, s, _ = q.shape
    return pl.pallas_call(
        functools.partial(_attn_kernel, tk=tk),
        grid=(b, hh, s // tq),
        in_specs=[
            pl.BlockSpec((1, 1, tq, QK_PAD), lambda bi, hi, qi: (bi, hi, qi, 0)),
            pl.BlockSpec((1, 1, s, QK_PAD), lambda bi, hi, qi: (bi, hi, 0, 0)),
            pl.BlockSpec((1, 1, s, V_HEAD), lambda bi, hi, qi: (bi, hi, 0, 0)),
        ],
        out_specs=pl.BlockSpec((1, tq, V_HEAD), lambda bi, hi, qi: (bi, qi, hi)),
        out_shape=jax.ShapeDtypeStruct((b, s, hh * V_HEAD), BF16),
        compiler_params=_cparams("arbitrary", "arbitrary", "arbitrary"),
        name="attention",
    )(q, k, v)


def _route_rows(logits):
    lane = lax.broadcasted_iota(jnp.int32, logits.shape, 1)
    lane_f = lane.astype(F32)
    neg = jnp.float32(-jnp.inf)
    big = jnp.float32(4 * LANES)
    gmask = (lane >= N_EXPERTS) & (lane < N_EXPERTS + N_GROUPS)
    gl = jnp.where(gmask, logits, neg)
    ge = jnp.exp(gl - jnp.max(gl, axis=-1, keepdims=True))
    gp = ge / jnp.sum(ge, axis=-1, keepdims=True)
    g_w = jnp.max(gp, axis=-1, keepdims=True)
    g_lane = jnp.min(jnp.where(gmask & (gp == g_w), lane_f, big), axis=-1, keepdims=True)
    grp = g_lane - N_EXPERTS
    e_lo = grp * EXPERTS_PER_GROUP
    emask = (lane_f >= e_lo) & (lane_f < e_lo + EXPERTS_PER_GROUP)
    el = jnp.where(emask, logits, neg)
    ee = jnp.exp(el - jnp.max(el, axis=-1, keepdims=True))
    ep = ee / jnp.sum(ee, axis=-1, keepdims=True)
    e1 = jnp.max(ep, axis=-1, keepdims=True)
    i1 = jnp.min(jnp.where(emask & (ep == e1), lane_f, big), axis=-1, keepdims=True)
    rest = emask & (lane_f != i1)
    ep2 = jnp.where(rest, ep, -1.0)
    e2 = jnp.max(ep2, axis=-1, keepdims=True)
    i2 = jnp.min(jnp.where(rest & (ep2 == e2), lane_f, big), axis=-1, keepdims=True)
    tot = e1 + e2
    w4 = jnp.where(lane_f == i1 - e_lo, g_w * (e1 / tot), jnp.where(lane_f == i2 - e_lo, g_w * (e2 / tot), 0.0))
    return grp, w4, lane_f


def _router_kernel(a_ref, wo_ref, x_ref, g1_ref, g_ref, sh_ref, sc_ref, w_ref, b_ref, tri_ref,
                   xn_ref, h_ref, meta_ref, cnt_ref):
    tm, d = x_ref.shape
    slab = tri_ref.shape[0]
    wst = w_ref[...]
    seen = jnp.zeros((1, LANES), F32)
    ranked = []
    for r in range(0, tm, slab):
        rows = slice(r, r + slab)
        xn = x_ref[rows] + g1_ref[0] * jnp.dot(a_ref[rows], wo_ref[...], preferred_element_type=F32)
        xn_ref[rows] = xn
        h = _norm_mod(xn, g_ref[...], sh_ref[0], sc_ref[0])
        hb = h.astype(BF16)
        rem = h - hb.astype(F32)
        hm = rem.astype(BF16)
        hl = (rem - hm.astype(F32)).astype(BF16)
        part = (jnp.dot(hb, wst, preferred_element_type=F32) + jnp.dot(hm, wst, preferred_element_type=F32)
                + jnp.dot(hl, wst, preferred_element_type=F32))
        logits = (part + pltpu.roll(part, LANES - ROUTER_LANES, 1)
                  + pltpu.roll(part, LANES - 2 * ROUTER_LANES, 1)) + b_ref[...]
        grp, w4, lane_f = _route_rows(logits)
        hi = w4.astype(BF16).astype(F32)
        mid = (w4 - hi).astype(BF16).astype(F32)
        lo = (w4 - hi - mid).astype(BF16).astype(F32)
        ext = hi + pltpu.roll(mid, EXPERTS_PER_GROUP, 1) + pltpu.roll(lo, 2 * EXPERTS_PER_GROUP, 1)
        h_ref[rows, :d] = hb
        h_ref[rows, d:] = ext.astype(h_ref.dtype)
        onehot = lane_f == grp
        ohf = jnp.where(onehot, 1.0, 0.0)
        rank = jnp.dot(tri_ref[...], ohf.astype(BF16), preferred_element_type=F32) + seen
        seen = seen + jnp.sum(ohf, axis=0, keepdims=True)
        ranked.append((rows, onehot, rank))
    counts = jnp.broadcast_to(seen, (8, LANES))
    starts = pltpu.roll(counts, 1, 1) + pltpu.roll(counts, 2, 1) + pltpu.roll(counts, 3, 1)
    for rows, onehot, rank in ranked:
        pos = jnp.sum(jnp.where(onehot, rank - 1.0 + starts[0:1], 0.0), axis=-1, keepdims=True)
        meta_ref[rows] = jnp.broadcast_to(pos, (rows.stop - rows.start, LANES))
    cnt_ref[0] = counts


def moe_router(a, w_o, x, gate, g, sh, sc, w_r, b_r, seq, tm):
    t, d = x.shape
    k = a.shape[1]
    per = seq // tm
    nb = sh.shape[0]
    slab = min(MOE_SLAB, tm)
    tri = jnp.tril(jnp.ones((slab, slab), BF16))
    return pl.pallas_call(
        _router_kernel,
        grid=(t // tm,),
        in_specs=[
            pl.BlockSpec((tm, k), lambda i: (i, 0)),
            pl.BlockSpec((k, d), lambda i: (0, 0)),
            pl.BlockSpec((tm, d), lambda i: (i, 0)),
            pl.BlockSpec((1, 1, d), lambda i: (i // per, 0, 0)),
            pl.BlockSpec((1, d), lambda i: (0, 0)),
            pl.BlockSpec((1, 1, d), lambda i: (i // per, 0, 0)),
            pl.BlockSpec((1, 1, d), lambda i: (i // per, 0, 0)),
            pl.BlockSpec((d, LANES), lambda i: (0, 0)),
            pl.BlockSpec((1, LANES), lambda i: (0, 0)),
            pl.BlockSpec((slab, slab), lambda i: (0, 0)),
        ],
        out_specs=[
            pl.BlockSpec((tm, d), lambda i: (i, 0)),
            pl.BlockSpec((tm, d + LANES), lambda i: (i, 0)),
            pl.BlockSpec((tm, LANES), lambda i: (i, 0)),
            pl.BlockSpec((1, 8, LANES), lambda i: (i, 0, 0)),
        ],
        out_shape=[
            jax.ShapeDtypeStruct((t, d), F32),
            jax.ShapeDtypeStruct((t, d + LANES), BF16),
            jax.ShapeDtypeStruct((t, LANES), F32),
            jax.ShapeDtypeStruct((t // tm, 8, LANES), F32),
        ],
        compiler_params=_cparams("arbitrary"),
        name="moe_router",
    )(a, w_o, x, gate.reshape(nb, 1, d), g.reshape(1, d), sh.reshape(nb, 1, d), sc.reshape(nb, 1, d), w_r, b_r, tri)


def _moe_kernel(st_ref, en_ref, h_ref, meta_ref, posr_ref, wg_ref, wu_ref, wd_ref, x_ref, gate_ref, o_ref,
                hs_ref, y_ref):
    i = pl.program_id(0)
    g = pl.program_id(1)
    tm, d = x_ref.shape
    slab = min(MOE_SLAB, tm)

    @pl.when(g == 0)
    def _():
        pos_row = posr_ref[0]
        for r in range(0, tm, slab):
            rows = lax.broadcasted_iota(jnp.int32, (slab, tm), 0).astype(F32) + float(r)
            perm = jnp.where(rows == pos_row, 1.0, 0.0).astype(BF16)
            hs_ref[r:r + slab] = jnp.dot(perm, h_ref[...], preferred_element_type=F32).astype(BF16)
        y_ref[...] = jnp.zeros_like(y_ref)

    start = st_ref[i * N_GROUPS + g]
    end = en_ref[i * N_GROUPS + g]
    shift = MOE_CHUNK.bit_length() - 1
    c_lo = lax.shift_right_logical(start, shift)
    c_hi = jnp.where(end > start, lax.shift_right_logical(end + (MOE_CHUNK - 1), shift), c_lo)

    def chunk(c, carry):
        r0 = pl.multiple_of(c * MOE_CHUNK, MOE_CHUNK)
        hx = hs_ref[pl.ds(r0, MOE_CHUNK), :]
        hs = hx[:, :d]
        ext = hx[:, d:].astype(F32)
        w4 = (ext + pltpu.roll(ext, LANES - EXPERTS_PER_GROUP, 1)
              + pltpu.roll(ext, LANES - 2 * EXPERTS_PER_GROUP, 1))
        rowid = r0 + lax.broadcasted_iota(jnp.int32, w4.shape, 0)
        w4 = jnp.where((rowid >= start) & (rowid < end), w4, 0.0)
        lane = lax.broadcasted_iota(jnp.int32, w4.shape, 1)
        acc = jnp.zeros((MOE_CHUNK, d), F32)
        for j in range(EXPERTS_PER_GROUP):
            a = jnp.dot(hs, wg_ref[j], preferred_element_type=F32)
            u = jnp.dot(hs, wu_ref[j], preferred_element_type=F32)
            wj = jnp.sum(jnp.where(lane == j, w4, 0.0), axis=-1, keepdims=True)
            act = (a * jax.nn.sigmoid(a)) * u * wj
            acc = acc + jnp.dot(act.astype(BF16), wd_ref[j], preferred_element_type=F32)
        y_ref[pl.ds(r0, MOE_CHUNK), :] += acc
        return carry

    lax.fori_loop(c_lo, c_hi, chunk, 0)

    @pl.when(g == N_GROUPS - 1)
    def _():
        yb = y_ref[...].astype(BF16)
        for r in range(0, tm, slab):
            pos_col = meta_ref[r:r + slab, 0:1]
            cols = lax.broadcasted_iota(jnp.int32, (slab, tm), 1).astype(F32)
            perm_t = jnp.where(cols == pos_col, 1.0, 0.0).astype(BF16)
            y = jnp.dot(perm_t, yb, preferred_element_type=F32)
            o_ref[r:r + slab] = x_ref[r:r + slab] + gate_ref[0] * y


def moe_experts(h_ext, meta, counts, wg, wu, wd, x, gate, seq, tm):
    t, d = x.shape
    f = wg.shape[2]
    per = seq // tm
    nb = gate.shape[0]
    ntiles = t // tm
    cnt = counts[:, 0, :N_GROUPS].astype(jnp.int32)
    ends = jnp.cumsum(cnt, axis=1)
    starts = ends - cnt
    pos_row = meta[:, 0].reshape(ntiles, 1, tm)
    epg = EXPERTS_PER_GROUP
    grid_spec = pltpu.PrefetchScalarGridSpec(
        num_scalar_prefetch=2,
        grid=(ntiles, N_GROUPS),
        in_specs=[
            pl.BlockSpec((tm, d + LANES), lambda i, g, st, en: (i, 0)),
            pl.BlockSpec((tm, LANES), lambda i, g, st, en: (i, 0)),
            pl.BlockSpec((1, 1, tm), lambda i, g, st, en: (i, 0, 0)),
            pl.BlockSpec((epg, d, f), lambda i, g, st, en: (g, 0, 0)),
            pl.BlockSpec((epg, d, f), lambda i, g, st, en: (g, 0, 0)),
            pl.BlockSpec((epg, f, d), lambda i, g, st, en: (g, 0, 0)),
            pl.BlockSpec((tm, d), lambda i, g, st, en: (i, 0)),
            pl.BlockSpec((1, 1, d), lambda i, g, st, en: (i // per, 0, 0)),
        ],
        out_specs=pl.BlockSpec((tm, d), lambda i, g, st, en: (i, 0)),
        scratch_shapes=[pltpu.VMEM((tm, d + LANES), BF16), pltpu.VMEM((tm, d), F32)],
    )
    return pl.pallas_call(
        _moe_kernel,
        grid_spec=grid_spec,
        out_shape=jax.ShapeDtypeStruct((t, d), F32),
        compiler_params=_cparams("arbitrary", "arbitrary"),
        name="moe_experts",
    )(starts.reshape(-1), ends.reshape(-1), h_ext, meta, pos_row, wg, wu, wd, x, gate.reshape(nb, 1, d))


def _filter_kernel(z_ref, w1_ref, b1_ref, f1_ref, w2_ref, b2_ref, f2_ref, w3_ref, b3_ref, f3_ref,
                   w4_ref, delta_ref, o_ref, hdn_ref):
    dot = functools.partial(jnp.dot, precision=HIGHEST, preferred_element_type=F32)

    @pl.when(pl.program_id(0) == 0)
    def _():
        hdn = jnp.sin(f1_ref[...] * (dot(z_ref[...], w1_ref[...]) + b1_ref[...]))
        hdn = jnp.sin(f2_ref[...] * (dot(hdn, w2_ref[...]) + b2_ref[...]))
        hdn_ref[...] = jnp.sin(f3_ref[...] * (dot(hdn, w3_ref[...]) + b3_ref[...]))

    h = dot(hdn_ref[...], w4_ref[...])
    t = z_ref[:, 0:1]
    h = h * jnp.exp(-t * delta_ref[...])
    o_ref[...] = (h / (jnp.sum(jnp.abs(h), axis=0, keepdims=True) + EPS)).astype(o_ref.dtype)


def hyena_filter(z, w1, b1, f1, w2, b2, f2, w3, b3, f3, w4, deltas, tn=256):
    length, emb = z.shape
    width = w2.shape[0]
    n = w4.shape[1]
    full = lambda shape: pl.BlockSpec(shape, lambda j: (0,) * len(shape))
    vec = lambda v: v.reshape(1, -1)
    return pl.pallas_call(
        _filter_kernel,
        grid=(n // tn,),
        in_specs=[
            full((length, emb)),
            full((emb, width)), full((1, width)), full((1, width)),
            full((width, width)), full((1, width)), full((1, width)),
            full((width, width)), full((1, width)), full((1, width)),
            pl.BlockSpec((width, tn), lambda j: (0, j)),
            pl.BlockSpec((1, tn), lambda j: (0, j)),
        ],
        out_specs=pl.BlockSpec((length, tn), lambda j: (0, j)),
        out_shape=jax.ShapeDtypeStruct((length, n), BF16),
        scratch_shapes=[pltpu.VMEM((length, width), F32)],
        compiler_params=_cparams("arbitrary"),
        name="hyena_filter",
    )(z, w1, vec(b1), vec(f1), w2, vec(b2), vec(f2), w3, vec(b3), vec(f3), w4, vec(deltas))


def _bmm_kernel(a_ref, x_ref, o_ref):
    o_ref[0] = jnp.dot(a_ref[...], x_ref[0], preferred_element_type=F32).astype(o_ref.dtype)


def block_dft(a, x, col0, ncols, out_dtype, tn=512, name="block_dft"):
    m, k = a.shape
    g = x.shape[0]
    c0 = col0 // tn
    return pl.pallas_call(
        _bmm_kernel,
        grid=(g, ncols // tn),
        in_specs=[
            pl.BlockSpec((m, k), lambda gi, j: (0, 0)),
            pl.BlockSpec((1, k, tn), lambda gi, j: (gi, 0, c0 + j)),
        ],
        out_specs=pl.BlockSpec((1, m, tn), lambda gi, j: (gi, 0, j)),
        out_shape=jax.ShapeDtypeStruct((g, m, ncols), out_dtype),
        compiler_params=_cparams("arbitrary", "arbitrary"),
        name=name,
    )(a, x)


CONV_HALO = 16


def _hyena_conv_kernel(fwd_ref, inv_ref, u_ref, g_ref, cw_ref, cb_ref, p_ref, q_ref, r_ref, bias_ref,
                       o_ref, uf_ref, us_ref, *, nblk, conv_u):
    s = u_ref.shape[1]
    tb = s // nblk

    def short_conv_block(ref, i, which):
        lo = max(i * tb - CONV_HALO, 0)
        hi = min((i + 1) * tb + CONV_HALO, s)
        x = ref[0, lo:hi].astype(F32)
        n = hi - lo
        prev = pltpu.roll(x, 1, 0)
        nxt = pltpu.roll(x, n - 1, 0)
        if lo == 0:
            prev = jnp.where(lax.broadcasted_iota(jnp.int32, x.shape, 0) == 0, 0.0, prev)
        if hi == s:
            nxt = jnp.where(lax.broadcasted_iota(jnp.int32, x.shape, 0) == n - 1, 0.0, nxt)
        w = cw_ref[which]
        y = prev * w[0:1] + x * w[1:2] + nxt * w[2:3] + cb_ref[which]
        off = i * tb - lo
        return y[off:off + tb]

    for j in range(nblk):
        rows = slice(j * tb, (j + 1) * tb)
        ub = short_conv_block(u_ref, j, 0).astype(BF16) if conv_u else u_ref[0, rows]
        us_ref[rows] = ub
        uf_ref[j] = jnp.dot(fwd_ref[...], ub, preferred_element_type=F32).astype(uf_ref.dtype)

    for i in range(nblk):
        re_acc = None
        im_acc = None
        for j in range(nblk):
            lag = i - j + nblk - 1
            re = uf_ref[j, :tb]
            im = uf_ref[j, tb:]
            p = p_ref[lag]
            q = q_ref[lag]
            r = r_ref[lag]
            tre = re * p - im * q
            tim = re * q + im * r
            re_acc = tre if re_acc is None else re_acc + tre
            im_acc = tim if im_acc is None else im_acc + tim
        yf = jnp.concatenate([re_acc, im_acc], axis=0)
        y = jnp.dot(inv_ref[...], yf, preferred_element_type=F32)
        rows = slice(i * tb, (i + 1) * tb)
        gate = short_conv_block(g_ref, i, 1)
        o_ref[0, rows] = (gate * (y + us_ref[rows].astype(F32) * bias_ref[...])).astype(o_ref.dtype)


def hyena_conv(fwd, inv, u_arr, u_col0, conv_u, g_arr, g_col0, cw, cb, p, q, r, p_col0, bias, nblk, name, tn=256):
    nb, s, _ = u_arr.shape
    d = bias.shape[0]
    tb = s // nblk
    nlag = p.shape[0]
    uc0, gc0, pc0 = u_col0 // tn, g_col0 // tn, p_col0 // tn
    once = pl.Buffered(1)
    hspec = pl.BlockSpec((nlag, tb, tn), lambda ci, bi: (0, 0, pc0 + ci), pipeline_mode=once)
    return pl.pallas_call(
        functools.partial(_hyena_conv_kernel, nblk=nblk, conv_u=conv_u),
        grid=(d // tn, nb),
        in_specs=[
            pl.BlockSpec(fwd.shape, lambda ci, bi: (0, 0), pipeline_mode=once),
            pl.BlockSpec(inv.shape, lambda ci, bi: (0, 0), pipeline_mode=once),
            pl.BlockSpec((1, s, tn), lambda ci, bi: (bi, 0, uc0 + ci)),
            pl.BlockSpec((1, s, tn), lambda ci, bi: (bi, 0, gc0 + ci)),
            pl.BlockSpec((2, 3, tn), lambda ci, bi: (0, 0, ci)),
            pl.BlockSpec((2, 1, tn), lambda ci, bi: (0, 0, ci)),
            hspec, hspec, hspec,
            pl.BlockSpec((1, tn), lambda ci, bi: (0, ci)),
        ],
        out_specs=pl.BlockSpec((1, s, tn), lambda ci, bi: (bi, 0, ci)),
        out_shape=jax.ShapeDtypeStruct((nb, s, d), BF16),
        scratch_shapes=[pltpu.VMEM((nblk, 2 * tb, tn), BF16), pltpu.VMEM((s, tn), BF16)],
        compiler_params=_cparams("arbitrary", "arbitrary"),
        name=name,
    )(fwd, inv, u_arr, g_arr, cw, cb, p, q, r, bias.reshape(1, d))


def _dft_matrices(tb):
    n = 2 * tb
    k = np.arange(tb, dtype=np.float64)[:, None]
    t = np.arange(tb, dtype=np.float64)[None, :]
    ang = 2.0 * np.pi * k * t / n
    fre = np.cos(ang)
    fim = -np.sin(ang)
    fim[0, :] = np.cos(np.pi * t[0])
    fwd = np.concatenate([fre, fim], axis=0)
    wre = np.full((tb, 1), 2.0)
    wre[0, 0] = 1.0
    ire = wre * np.cos(ang) / n
    iim = -2.0 * np.sin(ang) / n
    iim[0, :] = np.cos(np.pi * t[0]) / n
    inv = np.concatenate([ire, iim], axis=0).T
    sign = np.where(np.arange(tb) % 2 == 0, 1.0, -1.0)
    return jnp.asarray(fwd, BF16), jnp.asarray(inv, BF16), jnp.asarray(sign, F32)


def _fspec_kernel(ef_ref, eb_ref, tf_ref, tb_ref, s_ref, p_ref, q_ref, r_ref):
    nblk = ef_ref.shape[0]
    sgn = s_ref[...]
    shape = ef_ref.shape[2:]
    row0 = (lax.broadcasted_iota(jnp.int32, shape, 0) == 0) & (pl.program_id(1) == 0)

    def block(m):
        if m >= 0:
            return ef_ref[m, 0].astype(F32), ef_ref[m, 1].astype(F32), tf_ref[m]
        mu = -m
        tap = tb_ref[mu - 1]
        xre = eb_ref[mu - 1, 0].astype(F32) - tap
        xim = eb_ref[mu - 1, 1].astype(F32) - jnp.where(row0, tap, 0.0)
        b0 = tb_ref[mu] if mu < nblk else jnp.zeros_like(tap)
        return b0 + sgn * xre, jnp.where(row0, b0 + xim, -sgn * xim), b0

    prev = block(-nblk)
    for m in range(-(nblk - 1), nblk):
        cur = block(m)
        hre = cur[0] + sgn * (prev[0] - prev[2])
        him = cur[1] + sgn * (prev[1] - jnp.where(row0, prev[2], 0.0))
        lag = m + nblk - 1
        p_ref[lag] = hre.astype(p_ref.dtype)
        q_ref[lag] = jnp.where(row0, 0.0, him).astype(q_ref.dtype)
        r_ref[lag] = jnp.where(row0, him, hre).astype(r_ref.dtype)
        prev = cur


def filter_spectra(filt, fwd, sign, nblk, tb, tf=256, tn=256):
    d = D_MODEL
    tf = min(tf, tb)
    ncol = filt.shape[1]
    fb = filt.reshape(nblk, tb, ncol)
    e = block_dft(fwd, fb, 0, ncol, BF16, name="filter_dft").reshape(nblk, 2, tb, ncol)
    taps = fb[:, 0:1, :].astype(F32)
    nlag = 2 * nblk - 1
    per = d // tn
    espec = lambda direction: pl.BlockSpec(
        (nblk, 2, tf, tn), lambda o, fi, ci: (0, 0, fi, (2 * o + direction) * per + ci))
    tspec = lambda direction: pl.BlockSpec(
        (nblk, 1, tn), lambda o, fi, ci: (0, 0, (2 * o + direction) * per + ci))
    ospec = pl.BlockSpec((nlag, tf, tn), lambda o, fi, ci: (0, fi, o * per + ci))
    oshape = jax.ShapeDtypeStruct((nlag, tb, HYENA_ORDER * d), BF16)
    return pl.pallas_call(
        _fspec_kernel,
        grid=(HYENA_ORDER, tb // tf, per),
        in_specs=[espec(0), espec(1), tspec(0), tspec(1), pl.BlockSpec((tf, 1), lambda o, fi, ci: (fi, 0))],
        out_specs=[ospec, ospec, ospec],
        out_shape=[oshape, oshape, oshape],
        compiler_params=_cparams("arbitrary", "arbitrary", "arbitrary"),
        name="filter_spectra",
    )(e, e, taps, taps, sign.reshape(tb, 1))


def kernel(x, c, positions, ada_w, ada_b, norm_mix_g, norm_ffn_g, mla_w_down, mla_q_a_g, mla_kv_a_g, mla_w_uq, mla_w_ukv, mla_q_norm_g, mla_k_norm_g, mla_w_o, hy_w_in, hy_b_in, hy_conv_w, hy_conv_b, hy_f_w1, hy_f_b1, hy_f_freq1, hy_f_w2, hy_f_b2, hy_f_freq2, hy_f_w3, hy_f_b3, hy_f_freq3, hy_f_w4, hy_filt_bias, hy_w_out, moe_wg, moe_bg, moe_we, moe_be, moe_w_gate, moe_w_up, moe_w_down):
    batch, seq, d = x.shape
    t = batch * seq
    hh = MLA_HEADS
    xf = x.reshape(t, d)

    mod = adaln(c, ada_w, ada_b)

    def mods(i):
        return [mod[i, :, j * d:(j + 1) * d] for j in range(6)]

    def moe_layer(a, w_o, xin, g1, i, sh2, sc2, g2):
        w_r = jnp.concatenate([moe_we[i], moe_wg[i], jnp.zeros((d, ROUTER_LANES - N_EXPERTS - N_GROUPS), F32)], axis=1)
        w_hi = w_r.astype(BF16)
        w_rem = w_r - w_hi.astype(F32)
        w_mid = w_rem.astype(BF16)
        w_lo = (w_rem - w_mid.astype(F32)).astype(BF16)
        w_r = jnp.concatenate([w_hi, w_mid, w_lo, jnp.zeros((d, LANES - 3 * ROUTER_LANES), BF16)], axis=1)
        b_r = jnp.concatenate([moe_be[i], moe_bg[i], jnp.zeros((LANES - N_EXPERTS - N_GROUPS,), F32)]).reshape(1, LANES)
        tm = min(MOE_TILE, seq)
        xn, h_ext, meta, counts = moe_router(a, w_o.astype(BF16), xin, g1, norm_ffn_g[i], sh2, sc2, w_r, b_r, seq, tm)
        return moe_experts(h_ext, meta, counts, moe_w_gate[i].astype(BF16), moe_w_up[i].astype(BF16),
                           moe_w_down[i].astype(BF16), xn, g2, seq, tm)

    sh1, sc1, g1, sh2, sc2, g2 = mods(0)
    lat_w = mla_w_down[0]
    w_dn = jnp.concatenate([lat_w, lat_w[:, Q_LORA + KV_LORA:]], axis=1).astype(BF16)

    inv_freq = 1.0 / (ROPE_THETA ** (jnp.arange(0, QK_ROPE, 2, dtype=F32) / QK_ROPE))
    zeros_half = jnp.zeros((LANES - QK_ROPE,), F32)
    ones_half = jnp.ones((ROPE_HALF,), F32)
    freq_lanes = jnp.concatenate([inv_freq, inv_freq, zeros_half])
    ang = positions.reshape(t, 1).astype(F32) * freq_lanes[None, :]
    cos_t = jnp.cos(ang) * jnp.concatenate([ones_half, ones_half, zeros_half])[None, :]
    sin_t = jnp.sin(ang) * jnp.concatenate([-ones_half, ones_half, zeros_half])[None, :]

    wq = mla_w_uq[0].reshape(Q_LORA, hh, QK_HEAD)
    wq = jnp.concatenate([wq, wq[:, :, QK_NOPE:]], axis=-1)
    wq = wq.reshape(Q_LORA, hh * QK_PAD).astype(BF16)
    wkv = mla_w_ukv[0].reshape(KV_LORA, hh, QK_NOPE + V_HEAD)
    wk = wkv[:, :, :QK_NOPE].reshape(KV_LORA, hh * QK_NOPE).astype(BF16)
    wv = wkv[:, :, QK_NOPE:].reshape(KV_LORA, hh * V_HEAD).astype(BF16)
    gq = jnp.concatenate([mla_q_norm_g[0], mla_q_norm_g[0][QK_NOPE:]]).reshape(1, QK_PAD)
    gkn = mla_k_norm_g[0][:QK_NOPE].reshape(1, LANES)
    gkp = jnp.concatenate([mla_k_norm_g[0][QK_NOPE:], mla_k_norm_g[0][QK_NOPE:]]).reshape(1, LANES)
    q, k, v = mla_qkv(xf, norm_mix_g[0], sh1, sc1, w_dn, cos_t, sin_t, mla_q_a_g[0].reshape(1, Q_LORA), mla_kv_a_g[0].reshape(1, KV_LORA),
                      wq, wk, wv, gq, gkn, gkp, batch, seq)
    o = attention(q, k, v)
    xf = moe_layer(o.reshape(t, hh * V_HEAD), mla_w_o[0], xf, g1, 0, sh2, sc2, g2)

    sh1, sc1, g1, sh2, sc2, g2 = mods(1)
    nblk = CONV_BLOCKS
    tb = seq // nblk
    fwd, inv, sign = _dft_matrices(tb)

    tt = jnp.linspace(0.0, 1.0, seq, dtype=F32)[:, None]
    wfreq = 2.0 * math.pi * jnp.arange(seq, dtype=F32)[:, None] / seq
    fr = jnp.linspace(1e-4, FILTER_BANDS - 1, FILTER_BANDS, dtype=F32)[None, :]
    z = jnp.concatenate([tt, jnp.cos(fr * wfreq), -jnp.sin(fr * wfreq)], axis=-1)
    deltas = jnp.abs(jnp.linspace(math.log(FAST_DECAY) / DECAY_TARGET, math.log(SLOW_DECAY) / DECAY_TARGET, d, dtype=F32))
    filt = hyena_filter(z, hy_f_w1[0], hy_f_b1[0], hy_f_freq1[0], hy_f_w2[0], hy_f_b2[0], hy_f_freq2[0],
                        hy_f_w3[0], hy_f_b3[0], hy_f_freq3[0], hy_f_w4[0], jnp.tile(deltas, HYENA_ORDER * 2))
    p, qc, r = filter_spectra(filt, fwd, sign, nblk, tb)

    u = norm_mod_matmul(xf, norm_mix_g[1], sh1, sc1, hy_w_in[0].astype(BF16), hy_b_in[0], seq, BF16, name="hy_in")
    u3 = u.reshape(batch, seq, 3 * d)
    cw3 = hy_conv_w[0].reshape(3, 3, d)
    cb3 = hy_conv_b[0].reshape(3, 1, d)

    zsrc, zcol = u3, 2 * d
    for order in range(HYENA_ORDER):
        cw = jnp.stack([cw3[:, 2], cw3[:, order]], axis=0)
        cb = jnp.stack([cb3[2], cb3[order]], axis=0)
        zsrc = hyena_conv(fwd, inv, zsrc, zcol, order == 0, u3, order * d, cw, cb, p, qc, r, order * d,
                          hy_filt_bias[0, order], nblk, name=f"hyena_conv{order}")
        zcol = 0
    xf = moe_layer(zsrc.reshape(t, d), hy_w_out[0], xf, g1, 1, sh2, sc2, g2)
    return xf.reshape(batch, seq, d)
```

```python
import functools
import math

import jax
import jax.numpy as jnp
import numpy as np
from jax import lax
from jax.experimental import pallas as pl
from jax.experimental.pallas import tpu as pltpu

F32 = jnp.float32
BF16 = jnp.bfloat16
HIGHEST = lax.Precision.HIGHEST

D_MODEL = 1024
MLA_HEADS = 8
Q_LORA = 256
KV_LORA = 128
QK_NOPE = 128
QK_ROPE = 64
QK_HEAD = QK_NOPE + QK_ROPE
V_HEAD = 128
ROPE_HALF = QK_ROPE // 2
ROPE_THETA = 10000.0
HYENA_ORDER = 2
FILTER_EMB = 33
FILTER_BANDS = (FILTER_EMB - 1) // 2
FAST_DECAY = 0.3
SLOW_DECAY = 1.5
DECAY_TARGET = 1e-2
N_GROUPS = 4
EXPERTS_PER_GROUP = 4
N_EXPERTS = N_GROUPS * EXPERTS_PER_GROUP
D_EXPERT = 256
EPS = 1e-6

LANES = 128
QK_PAD = 2 * LANES
CONV_BLOCKS = 8
ROUTER_LANES = 32
MOE_TILE = 1024
MOE_CHUNK = 128
MOE_SLAB = 256
QKV_SLAB = 256
VMEM_LIMIT = 56 * 1024 * 1024


def _cparams(*sem):
    return pltpu.CompilerParams(dimension_semantics=sem, vmem_limit_bytes=VMEM_LIMIT)


def _adaln_kernel(c_ref, w_ref, b_ref, o_ref):
    c = c_ref[...]
    ca = c * jax.nn.sigmoid(c)
    o_ref[0] = jnp.dot(ca, w_ref[0], precision=HIGHEST, preferred_element_type=F32) + b_ref[0]


def adaln(c, ada_w, ada_b, tn=1536):
    depth, d, n = ada_w.shape
    b = c.shape[0]
    return pl.pallas_call(
        _adaln_kernel,
        grid=(depth, n // tn),
        in_specs=[
            pl.BlockSpec((b, d), lambda i, j: (0, 0)),
            pl.BlockSpec((1, d, tn), lambda i, j: (i, 0, j)),
            pl.BlockSpec((1, 1, tn), lambda i, j: (i, 0, j)),
        ],
        out_specs=pl.BlockSpec((1, b, tn), lambda i, j: (i, 0, j)),
        out_shape=jax.ShapeDtypeStruct((depth, b, n), F32),
        compiler_params=_cparams("arbitrary", "arbitrary"),
        name="adaln",
    )(c, ada_w, ada_b.reshape(depth, 1, n))


def _norm_mod(x, g, sh, sc):
    ms = jnp.mean(x * x, axis=-1, keepdims=True)
    y = x * lax.rsqrt(ms + EPS) * g
    return y * (1.0 + sc) + sh


def _nmm_kernel(x_ref, g_ref, sh_ref, sc_ref, w_ref, b_ref, o_ref):
    h = _norm_mod(x_ref[...], g_ref[...], sh_ref[0], sc_ref[0])
    o = jnp.dot(h.astype(BF16), w_ref[...], preferred_element_type=F32) + b_ref[...]
    o_ref[...] = o.astype(o_ref.dtype)


def norm_mod_matmul(x, g, sh, sc, w, b, seq, out_dtype, tm=512, name="nmm"):
    t, d = x.shape
    n = w.shape[1]
    per = seq // tm
    nb = sh.shape[0]
    return pl.pallas_call(
        _nmm_kernel,
        grid=(t // tm,),
        in_specs=[
            pl.BlockSpec((tm, d), lambda i: (i, 0)),
            pl.BlockSpec((1, d), lambda i: (0, 0)),
            pl.BlockSpec((1, 1, d), lambda i: (i // per, 0, 0)),
            pl.BlockSpec((1, 1, d), lambda i: (i // per, 0, 0)),
            pl.BlockSpec((d, n), lambda i: (0, 0)),
            pl.BlockSpec((1, n), lambda i: (0, 0)),
        ],
        out_specs=pl.BlockSpec((tm, n), lambda i: (i, 0)),
        out_shape=jax.ShapeDtypeStruct((t, n), out_dtype),
        compiler_params=_cparams("arbitrary"),
        name=name,
    )(x, g.reshape(1, d), sh.reshape(nb, 1, d), sc.reshape(nb, 1, d), w, b.reshape(1, n))


def _rms(x, n):
    return x * lax.rsqrt(jnp.sum(x * x, axis=-1, keepdims=True) * (1.0 / n) + EPS)


def _qkv_kernel(x_ref, ng_ref, sh_ref, sc_ref, wdn_ref, cos_ref, sin_ref, qag_ref, kvag_ref, wq_ref, wk_ref, wv_ref,
                gq_ref, gkn_ref, gkp_ref, q_ref, k_ref, v_ref):
    tm = x_ref.shape[0]
    slab = min(QKV_SLAB, tm)
    gq = gq_ref[...]
    scale = QK_HEAD ** -0.5 * math.log2(math.e)
    for r in range(0, tm, slab):
        rows = slice(r, r + slab)
        h = _norm_mod(x_ref[rows], ng_ref[...], sh_ref[0], sc_ref[0])
        lat = jnp.dot(h.astype(BF16), wdn_ref[...], preferred_element_type=F32)
        cq = lat[:, :Q_LORA]
        ckv = lat[:, Q_LORA:Q_LORA + KV_LORA]
        kpe = lat[:, Q_LORA + KV_LORA:]
        cqn = (_rms(cq, Q_LORA) * qag_ref[...]).astype(BF16)
        ckvn = (_rms(ckv, KV_LORA) * kvag_ref[...]).astype(BF16)
        qall = jnp.dot(cqn, wq_ref[...], preferred_element_type=F32)
        kall = jnp.dot(ckvn, wk_ref[...], preferred_element_type=F32)
        vall = jnp.dot(ckvn, wv_ref[...], preferred_element_type=F32)
        cos_t = cos_ref[rows]
        sin_t = sin_ref[rows]

        def rope(pe):
            return pe * cos_t + pltpu.roll(pe, ROPE_HALF, 1) * sin_t

        kp = rope(_rms(kpe, 2 * QK_ROPE) * gkp_ref[...])
        for h in range(MLA_HEADS):
            qh = qall[:, h * QK_PAD:(h + 1) * QK_PAD]
            qn = _rms(qh[:, :QK_NOPE], QK_NOPE) * gq[:, :QK_NOPE]
            qp = rope(_rms(qh[:, QK_NOPE:], 2 * QK_ROPE) * gq[:, QK_NOPE:])
            q_ref[0, h, rows] = (jnp.concatenate([qn, qp], axis=-1) * scale).astype(q_ref.dtype)
            kn = _rms(kall[:, h * QK_NOPE:(h + 1) * QK_NOPE], QK_NOPE) * gkn_ref[...]
            k_ref[0, h, rows] = jnp.concatenate([kn, kp], axis=-1).astype(k_ref.dtype)
            v_ref[0, h, rows] = vall[:, h * V_HEAD:(h + 1) * V_HEAD].astype(v_ref.dtype)


def mla_qkv(x, norm_g, sh, sc, w_dn, cos_t, sin_t, q_a_g, kv_a_g, wq, wk, wv, gq, gkn, gkp, batch, seq, tm=512):
    t, d = x.shape
    per = seq // tm
    hh = MLA_HEADS
    full = lambda shape: pl.BlockSpec(shape, lambda i: (0,) * len(shape))
    return pl.pallas_call(
        _qkv_kernel,
        grid=(t // tm,),
        in_specs=[
            pl.BlockSpec((tm, d), lambda i: (i, 0)),
            full((1, d)),
            pl.BlockSpec((1, 1, d), lambda i: (i // per, 0, 0)),
            pl.BlockSpec((1, 1, d), lambda i: (i // per, 0, 0)),
            full(w_dn.shape),
            pl.BlockSpec((tm, LANES), lambda i: (i, 0)),
            pl.BlockSpec((tm, LANES), lambda i: (i, 0)),
            full((1, Q_LORA)), full((1, KV_LORA)),
            full(wq.shape), full(wk.shape), full(wv.shape),
            full((1, QK_PAD)), full((1, LANES)), full((1, LANES)),
        ],
        out_specs=[
            pl.BlockSpec((1, hh, tm, QK_PAD), lambda i: (i // per, 0, i % per, 0)),
            pl.BlockSpec((1, hh, tm, QK_PAD), lambda i: (i // per, 0, i % per, 0)),
            pl.BlockSpec((1, hh, tm, V_HEAD), lambda i: (i // per, 0, i % per, 0)),
        ],
        out_shape=[
            jax.ShapeDtypeStruct((batch, hh, seq, QK_PAD), BF16),
            jax.ShapeDtypeStruct((batch, hh, seq, QK_PAD), BF16),
            jax.ShapeDtypeStruct((batch, hh, seq, V_HEAD), BF16),
        ],
        compiler_params=_cparams("arbitrary"),
        name="mla_qkv",
    )(x, norm_g.reshape(1, d), sh.reshape(batch, 1, d), sc.reshape(batch, 1, d), w_dn,
      cos_t, sin_t, q_a_g, kv_a_g, wq, wk, wv, gq, gkn, gkp)


def _attn_kernel(q_ref, k_ref, v_ref, o_ref, *, tk):
    q = q_ref[0, 0]
    tq = q.shape[0]
    nk = k_ref.shape[2] // tk

    def body(j, carry):
        m, l, acc = carry
        start = pl.multiple_of(j * tk, tk)
        ks = k_ref[0, 0, pl.ds(start, tk), :]
        vs = v_ref[0, 0, pl.ds(start, tk), :]
        s = lax.dot_general(q, ks, (((1,), (1,)), ((), ())), preferred_element_type=F32)
        m_new = jnp.maximum(m, jnp.max(s, axis=-1, keepdims=True))
        p = jnp.exp2(s - m_new)
        alpha = jnp.exp2(m - m_new)
        l = alpha * l + jnp.sum(p, axis=-1, keepdims=True)
        acc = alpha * acc + jnp.dot(p.astype(BF16), vs, preferred_element_type=F32)
        return m_new, l, acc

    m0 = jnp.full((tq, 1), -jnp.inf, F32)
    l0 = jnp.zeros((tq, 1), F32)
    acc0 = jnp.zeros((tq, V_HEAD), F32)
    _, l, acc = lax.fori_loop(0, nk, body, (m0, l0, acc0), unroll=True)
    o_ref[0] = (acc / l).astype(o_ref.dtype)


def attention(q, k, v, tq=4096, tk=256):
    tq = min(tq, q.shape[2])
    tk = min(tk, q.shape[2])
    b, hh, s, _ = q.shape
    return pl.pallas_call(
        functools.partial(_attn_kernel, tk=tk),
        grid=(b, hh, s // tq),
        in_specs=[
            pl.BlockSpec((1, 1, tq, QK_PAD), lambda bi, hi, qi: (bi, hi, qi, 0)),
            pl.BlockSpec((1, 1, s, QK_PAD), lambda bi, hi, qi: (bi, hi, 0, 0)),
            pl.BlockSpec((1, 1, s, V_HEAD), lambda bi, hi, qi: (bi, hi, 0, 0)),
        ],
        out_specs=pl.BlockSpec((1, tq, V_HEAD), lambda bi, hi, qi: (bi, qi, hi)),
        out_shape=jax.ShapeDtypeStruct((b, s, hh * V_HEAD), BF16),
        compiler_params=_cparams("arbitrary", "arbitrary", "arbitrary"),
        name="attention",
    )(q, k, v)


def _route_rows(logits):
    lane = lax.broadcasted_iota(jnp.int32, logits.shape, 1)
    lane_f = lane.astype(F32)
    neg = jnp.float32(-jnp.inf)
    big = jnp.float32(4 * LANES)
    gmask = (lane >= N_EXPERTS) & (lane < N_EXPERTS + N_GROUPS)
    gl = jnp.where(gmask, logits, neg)
    ge = jnp.exp(gl - jnp.max(gl, axis=-1, keepdims=True))
    gp = ge / jnp.sum(ge, axis=-1, keepdims=True)
    g_w = jnp.max(gp, axis=-1, keepdims=True)
    g_lane = jnp.min(jnp.where(gmask & (gp == g_w), lane_f, big), axis=-1, keepdims=True)
    grp = g_lane - N_EXPERTS
    e_lo = grp * EXPERTS_PER_GROUP
    emask = (lane_f >= e_lo) & (lane_f < e_lo + EXPERTS_PER_GROUP)
    el = jnp.where(emask, logits, neg)
    ee = jnp.exp(el - jnp.max(el, axis=-1, keepdims=True))
    ep = ee / jnp.sum(ee, axis=-1, keepdims=True)
    e1 = jnp.max(ep, axis=-1, keepdims=True)
    i1 = jnp.min(jnp.where(emask & (ep == e1), lane_f, big), axis=-1, keepdims=True)
    rest = emask & (lane_f != i1)
    ep2 = jnp.where(rest, ep, -1.0)
    e2 = jnp.max(ep2, axis=-1, keepdims=True)
    i2 = jnp.min(jnp.where(rest & (ep2 == e2), lane_f, big), axis=-1, keepdims=True)
    tot = e1 + e2
    w4 = jnp.where(lane_f == i1 - e_lo, g_w * (e1 / tot), jnp.where(lane_f == i2 - e_lo, g_w * (e2 / tot), 0.0))
    return grp, w4, lane_f


def _router_kernel(a_ref, wo_ref, x_ref, g1_ref, g_ref, sh_ref, sc_ref, w_ref, b_ref, tri_ref,
                   xn_ref, h_ref, meta_ref, cnt_ref):
    tm, d = x_ref.shape
    slab = tri_ref.shape[0]
    wst = w_ref[...]
    seen = jnp.zeros((1, LANES), F32)
    ranked = []
    for r in range(0, tm, slab):
        rows = slice(r, r + slab)
        xn = x_ref[rows] + g1_ref[0] * jnp.dot(a_ref[rows], wo_ref[...], preferred_element_type=F32)
        xn_ref[rows] = xn
        h = _norm_mod(xn, g_ref[...], sh_ref[0], sc_ref[0])
        hb = h.astype(BF16)
        rem = h - hb.astype(F32)
        hm = rem.astype(BF16)
        hl = (rem - hm.astype(F32)).astype(BF16)
        part = (jnp.dot(hb, wst, preferred_element_type=F32) + jnp.dot(hm, wst, preferred_element_type=F32)
                + jnp.dot(hl, wst, preferred_element_type=F32))
        logits = (part + pltpu.roll(part, LANES - ROUTER_LANES, 1)
                  + pltpu.roll(part, LANES - 2 * ROUTER_LANES, 1)) + b_ref[...]
        grp, w4, lane_f = _route_rows(logits)
        hi = w4.astype(BF16).astype(F32)
        mid = (w4 - hi).astype(BF16).astype(F32)
        lo = (w4 - hi - mid).astype(BF16).astype(F32)
        ext = hi + pltpu.roll(mid, EXPERTS_PER_GROUP, 1) + pltpu.roll(lo, 2 * EXPERTS_PER_GROUP, 1)
        h_ref[rows, :d] = hb
        h_ref[rows, d:] = ext.astype(h_ref.dtype)
        onehot = lane_f == grp
        ohf = jnp.where(onehot, 1.0, 0.0)
        rank = jnp.dot(tri_ref[...], ohf.astype(BF16), preferred_element_type=F32) + seen
        seen = seen + jnp.sum(ohf, axis=0, keepdims=True)
        ranked.append((rows, onehot, rank))
    counts = jnp.broadcast_to(seen, (8, LANES))
    starts = pltpu.roll(counts, 1, 1) + pltpu.roll(counts, 2, 1) + pltpu.roll(counts, 3, 1)
    for rows, onehot, rank in ranked:
        pos = jnp.sum(jnp.where(onehot, rank - 1.0 + starts[0:1], 0.0), axis=-1, keepdims=True)
        meta_ref[rows] = jnp.broadcast_to(pos, (rows.stop - rows.start, LANES))
    cnt_ref[0] = counts


def moe_router(a, w_o, x, gate, g, sh, sc, w_r, b_r, seq, tm):
    t, d = x.shape
    k = a.shape[1]
    per = seq // tm
    nb = sh.shape[0]
    slab = min(MOE_SLAB, tm)
    tri = jnp.tril(jnp.ones((slab, slab), BF16))
    return pl.pallas_call(
        _router_kernel,
        grid=(t // tm,),
        in_specs=[
            pl.BlockSpec((tm, k), lambda i: (i, 0)),
            pl.BlockSpec((k, d), lambda i: (0, 0)),
            pl.BlockSpec((tm, d), lambda i: (i, 0)),
            pl.BlockSpec((1, 1, d), lambda i: (i // per, 0, 0)),
            pl.BlockSpec((1, d), lambda i: (0, 0)),
            pl.BlockSpec((1, 1, d), lambda i: (i // per, 0, 0)),
            pl.BlockSpec((1, 1, d), lambda i: (i // per, 0, 0)),
            pl.BlockSpec((d, LANES), lambda i: (0, 0)),
            pl.BlockSpec((1, LANES), lambda i: (0, 0)),
            pl.BlockSpec((slab, slab), lambda i: (0, 0)),
        ],
        out_specs=[
            pl.BlockSpec((tm, d), lambda i: (i, 0)),
            pl.BlockSpec((tm, d + LANES), lambda i: (i, 0)),
            pl.BlockSpec((tm, LANES), lambda i: (i, 0)),
            pl.BlockSpec((1, 8, LANES), lambda i: (i, 0, 0)),
        ],
        out_shape=[
            jax.ShapeDtypeStruct((t, d), F32),
            jax.ShapeDtypeStruct((t, d + LANES), BF16),
            jax.ShapeDtypeStruct((t, LANES), F32),
            jax.ShapeDtypeStruct((t // tm, 8, LANES), F32),
        ],
        compiler_params=_cparams("arbitrary"),
        name="moe_router",
    )(a, w_o, x, gate.reshape(nb, 1, d), g.reshape(1, d), sh.reshape(nb, 1, d), sc.reshape(nb, 1, d), w_r, b_r, tri)


def _moe_kernel(st_ref, en_ref, h_ref, meta_ref, posr_ref, wg_ref, wu_ref, wd_ref, x_ref, gate_ref, o_ref,
                hs_ref, y_ref):
    i = pl.program_id(0)
    g = pl.program_id(1)
    tm, d = x_ref.shape
    slab = min(MOE_SLAB, tm)

    @pl.when(g == 0)
    def _():
        pos_row = posr_ref[0]
        for r in range(0, tm, slab):
            rows = lax.broadcasted_iota(jnp.int32, (slab, tm), 0).astype(F32) + float(r)
            perm = jnp.where(rows == pos_row, 1.0, 0.0).astype(BF16)
            hs_ref[r:r + slab] = jnp.dot(perm, h_ref[...], preferred_element_type=F32).astype(BF16)
        y_ref[...] = jnp.zeros_like(y_ref)

    start = st_ref[i * N_GROUPS + g]
    end = en_ref[i * N_GROUPS + g]
    shift = MOE_CHUNK.bit_length() - 1
    c_lo = lax.shift_right_logical(start, shift)
    c_hi = jnp.where(end > start, lax.shift_right_logical(end + (MOE_CHUNK - 1), shift), c_lo)

    def chunk(c, carry):
        r0 = pl.multiple_of(c * MOE_CHUNK, MOE_CHUNK)
        hx = hs_ref[pl.ds(r0, MOE_CHUNK), :]
        hs = hx[:, :d]
        ext = hx[:, d:].astype(F32)
        w4 = (ext + pltpu.roll(ext, LANES - EXPERTS_PER_GROUP, 1)
              + pltpu.roll(ext, LANES - 2 * EXPERTS_PER_GROUP, 1))
        rowid = r0 + lax.broadcasted_iota(jnp.int32, w4.shape, 0)
        w4 = jnp.where((rowid >= start) & (rowid < end), w4, 0.0)
        lane = lax.broadcasted_iota(jnp.int32, w4.shape, 1)
        acc = jnp.zeros((MOE_CHUNK, d), F32)
        for j in range(EXPERTS_PER_GROUP):
            a = jnp.dot(hs, wg_ref[j], preferred_element_type=F32)
            u = jnp.dot(hs, wu_ref[j], preferred_element_type=F32)
            wj = jnp.sum(jnp.where(lane == j, w4, 0.0), axis=-1, keepdims=True)
            act = (a * jax.nn.sigmoid(a)) * u * wj
            acc = acc + jnp.dot(act.astype(BF16), wd_ref[j], preferred_element_type=F32)
        y_ref[pl.ds(r0, MOE_CHUNK), :] += acc
        return carry

    lax.fori_loop(c_lo, c_hi, chunk, 0)

    @pl.when(g == N_GROUPS - 1)
    def _():
        yb = y_ref[...].astype(BF16)
        for r in range(0, tm, slab):
            pos_col = meta_ref[r:r + slab, 0:1]
            cols = lax.broadcasted_iota(jnp.int32, (slab, tm), 1).astype(F32)
            perm_t = jnp.where(cols == pos_col, 1.0, 0.0).astype(BF16)
            y = jnp.dot(perm_t, yb, preferred_element_type=F32)
            o_ref[r:r + slab] = x_ref[r:r + slab] + gate_ref[0] * y


def moe_experts(h_ext, meta, counts, wg, wu, wd, x, gate, seq, tm):
    t, d = x.shape
    f = wg.shape[2]
    per = seq // tm
    nb = gate.shape[0]
    ntiles = t // tm
    cnt = counts[:, 0, :N_GROUPS].astype(jnp.int32)
    ends = jnp.cumsum(cnt, axis=1)
    starts = ends - cnt
    pos_row = meta[:, 0].reshape(ntiles, 1, tm)
    epg = EXPERTS_PER_GROUP
    grid_spec = pltpu.PrefetchScalarGridSpec(
        num_scalar_prefetch=2,
        grid=(ntiles, N_GROUPS),
        in_specs=[
            pl.BlockSpec((tm, d + LANES), lambda i, g, st, en: (i, 0)),
            pl.BlockSpec((tm, LANES), lambda i, g, st, en: (i, 0)),
            pl.BlockSpec((1, 1, tm), lambda i, g, st, en: (i, 0, 0)),
            pl.BlockSpec((epg, d, f), lambda i, g, st, en: (g, 0, 0)),
            pl.BlockSpec((epg, d, f), lambda i, g, st, en: (g, 0, 0)),
            pl.BlockSpec((epg, f, d), lambda i, g, st, en: (g, 0, 0)),
            pl.BlockSpec((tm, d), lambda i, g, st, en: (i, 0)),
            pl.BlockSpec((1, 1, d), lambda i, g, st, en: (i // per, 0, 0)),
        ],
        out_specs=pl.BlockSpec((tm, d), lambda i, g, st, en: (i, 0)),
        scratch_shapes=[pltpu.VMEM((tm, d + LANES), BF16), pltpu.VMEM((tm, d), F32)],
    )
    return pl.pallas_call(
        _moe_kernel,
        grid_spec=grid_spec,
        out_shape=jax.ShapeDtypeStruct((t, d), F32),
        compiler_params=_cparams("arbitrary", "arbitrary"),
        name="moe_experts",
    )(starts.reshape(-1), ends.reshape(-1), h_ext, meta, pos_row, wg, wu, wd, x, gate.reshape(nb, 1, d))


def _filter_kernel(z_ref, w1_ref, b1_ref, f1_ref, w2_ref, b2_ref, f2_ref, w3_ref, b3_ref, f3_ref,
                   w4_ref, delta_ref, o_ref, hdn_ref):
    dot = functools.partial(jnp.dot, precision=HIGHEST, preferred_element_type=F32)

    @pl.when(pl.program_id(0) == 0)
    def _():
        hdn = jnp.sin(f1_ref[...] * (dot(z_ref[...], w1_ref[...]) + b1_ref[...]))
        hdn = jnp.sin(f2_ref[...] * (dot(hdn, w2_ref[...]) + b2_ref[...]))
        hdn_ref[...] = jnp.sin(f3_ref[...] * (dot(hdn, w3_ref[...]) + b3_ref[...]))

    h = dot(hdn_ref[...], w4_ref[...])
    t = z_ref[:, 0:1]
    h = h * jnp.exp(-t * delta_ref[...])
    o_ref[...] = (h / (jnp.sum(jnp.abs(h), axis=0, keepdims=True) + EPS)).astype(o_ref.dtype)


def hyena_filter(z, w1, b1, f1, w2, b2, f2, w3, b3, f3, w4, deltas, tn=256):
    length, emb = z.shape
    width = w2.shape[0]
    n = w4.shape[1]
    full = lambda shape: pl.BlockSpec(shape, lambda j: (0,) * len(shape))
    vec = lambda v: v.reshape(1, -1)
    return pl.pallas_call(
        _filter_kernel,
        grid=(n // tn,),
        in_specs=[
            full((length, emb)),
            full((emb, width)), full((1, width)), full((1, width)),
            full((width, width)), full((1, width)), full((1, width)),
            full((width, width)), full((1, width)), full((1, width)),
            pl.BlockSpec((width, tn), lambda j: (0, j)),
            pl.BlockSpec((1, tn), lambda j: (0, j)),
        ],
        out_specs=pl.BlockSpec((length, tn), lambda j: (0, j)),
        out_shape=jax.ShapeDtypeStruct((length, n), BF16),
        scratch_shapes=[pltpu.VMEM((length, width), F32)],
        compiler_params=_cparams("arbitrary"),
        name="hyena_filter",
    )(z, w1, vec(b1), vec(f1), w2, vec(b2), vec(f2), w3, vec(b3), vec(f3), w4, vec(deltas))


def _bmm_kernel(a_ref, x_ref, o_ref):
    o_ref[0] = jnp.dot(a_ref[...], x_ref[0], preferred_element_type=F32).astype(o_ref.dtype)


def block_dft(a, x, col0, ncols, out_dtype, tn=512, name="block_dft"):
    m, k = a.shape
    g = x.shape[0]
    c0 = col0 // tn
    return pl.pallas_call(
        _bmm_kernel,
        grid=(g, ncols // tn),
        in_specs=[
            pl.BlockSpec((m, k), lambda gi, j: (0, 0)),
            pl.BlockSpec((1, k, tn), lambda gi, j: (gi, 0, c0 + j)),
        ],
        out_specs=pl.BlockSpec((1, m, tn), lambda gi, j: (gi, 0, j)),
        out_shape=jax.ShapeDtypeStruct((g, m, ncols), out_dtype),
        compiler_params=_cparams("arbitrary", "arbitrary"),
        name=name,
    )(a, x)


CONV_HALO = 16


def _hyena_conv_kernel(fwd_ref, inv_ref, u_ref, g_ref, cw_ref, cb_ref, p_ref, q_ref, r_ref, bias_ref,
                       o_ref, uf_ref, us_ref, *, nblk, conv_u):
    s = u_ref.shape[1]
    tb = s // nblk

    def short_conv_block(ref, i, which):
        lo = max(i * tb - CONV_HALO, 0)
        hi = min((i + 1) * tb + CONV_HALO, s)
        x = ref[0, lo:hi].astype(F32)
        n = hi - lo
        prev = pltpu.roll(x, 1, 0)
        nxt = pltpu.roll(x, n - 1, 0)
        if lo == 0:
            prev = jnp.where(lax.broadcasted_iota(jnp.int32, x.shape, 0) == 0, 0.0, prev)
        if hi == s:
            nxt = jnp.where(lax.broadcasted_iota(jnp.int32, x.shape, 0) == n - 1, 0.0, nxt)
        w = cw_ref[which]
        y = prev * w[0:1] + x * w[1:2] + nxt * w[2:3] + cb_ref[which]
        off = i * tb - lo
        return y[off:off + tb]

    for j in range(nblk):
        rows = slice(j * tb, (j + 1) * tb)
        ub = short_conv_block(u_ref, j, 0).astype(BF16) if conv_u else u_ref[0, rows]
        us_ref[rows] = ub
        uf_ref[j] = jnp.dot(fwd_ref[...], ub, preferred_element_type=F32).astype(uf_ref.dtype)

    for i in range(nblk):
        re_acc = None
        im_acc = None
        for j in range(nblk):
            lag = i - j + nblk - 1
            re = uf_ref[j, :tb]
            im = uf_ref[j, tb:]
            p = p_ref[lag]
            q = q_ref[lag]
            r = r_ref[lag]
            tre = re * p - im * q
            tim = re * q + im * r
            re_acc = tre if re_acc is None else re_acc + tre
            im_acc = tim if im_acc is None else im_acc + tim
        yf = jnp.concatenate([re_acc, im_acc], axis=0)
        y = jnp.dot(inv_ref[...], yf, preferred_element_type=F32)
        rows = slice(i * tb, (i + 1) * tb)
        gate = short_conv_block(g_ref, i, 1)
        o_ref[0, rows] = (gate * (y + us_ref[rows].astype(F32) * bias_ref[...])).astype(o_ref.dtype)


def hyena_conv(fwd, inv, u_arr, u_col0, conv_u, g_arr, g_col0, cw, cb, p, q, r, p_col0, bias, nblk, name, tn=256):
    nb, s, _ = u_arr.shape
    d = bias.shape[0]
    tb = s // nblk
    nlag = p.shape[0]
    uc0, gc0, pc0 = u_col0 // tn, g_col0 // tn, p_col0 // tn
    once = pl.Buffered(1)
    hspec = pl.BlockSpec((nlag, tb, tn), lambda ci, bi: (0, 0, pc0 + ci), pipeline_mode=once)
    return pl.pallas_call(
        functools.partial(_hyena_conv_kernel, nblk=nblk, conv_u=conv_u),
        grid=(d // tn, nb),
        in_specs=[
            pl.BlockSpec(fwd.shape, lambda ci, bi: (0, 0), pipeline_mode=once),
            pl.BlockSpec(inv.shape, lambda ci, bi: (0, 0), pipeline_mode=once),
            pl.BlockSpec((1, s, tn), lambda ci, bi: (bi, 0, uc0 + ci)),
            pl.BlockSpec((1, s, tn), lambda ci, bi: (bi, 0, gc0 + ci)),
            pl.BlockSpec((2, 3, tn), lambda ci, bi: (0, 0, ci)),
            pl.BlockSpec((2, 1, tn), lambda ci, bi: (0, 0, ci)),
            hspec, hspec, hspec,
            pl.BlockSpec((1, tn), lambda ci, bi: (0, ci)),
        ],
        out_specs=pl.BlockSpec((1, s, tn), lambda ci, bi: (bi, 0, ci)),
        out_shape=jax.ShapeDtypeStruct((nb, s, d), BF16),
        scratch_shapes=[pltpu.VMEM((nblk, 2 * tb, tn), BF16), pltpu.VMEM((s, tn), BF16)],
        compiler_params=_cparams("arbitrary", "arbitrary"),
        name=name,
    )(fwd, inv, u_arr, g_arr, cw, cb, p, q, r, bias.reshape(1, d))


def _dft_matrices(tb):
    n = 2 * tb
    k = np.arange(tb, dtype=np.float64)[:, None]
    t = np.arange(tb, dtype=np.float64)[None, :]
    ang = 2.0 * np.pi * k * t / n
    fre = np.cos(ang)
    fim = -np.sin(ang)
    fim[0, :] = np.cos(np.pi * t[0])
    fwd = np.concatenate([fre, fim], axis=0)
    wre = np.full((tb, 1), 2.0)
    wre[0, 0] = 1.0
    ire = wre * np.cos(ang) / n
    iim = -2.0 * np.sin(ang) / n
    iim[0, :] = np.cos(np.pi * t[0]) / n
    inv = np.concatenate([ire, iim], axis=0).T
    sign = np.where(np.arange(tb) % 2 == 0, 1.0, -1.0)
    return jnp.asarray(fwd, BF16), jnp.asarray(inv, BF16), jnp.asarray(sign, F32)


def _fspec_kernel(ef_ref, eb_ref, tf_ref, tb_ref, s_ref, p_ref, q_ref, r_ref):
    nblk = ef_ref.shape[0]
    sgn = s_ref[...]
    shape = ef_ref.shape[2:]
    row0 = (lax.broadcasted_iota(jnp.int32, shape, 0) == 0) & (pl.program_id(1) == 0)

    def block(m):
        if m >= 0:
            return ef_ref[m, 0].astype(F32), ef_ref[m, 1].astype(F32), tf_ref[m]
        mu = -m
        tap = tb_ref[mu - 1]
        xre = eb_ref[mu - 1, 0].astype(F32) - tap
        xim = eb_ref[mu - 1, 1].astype(F32) - jnp.where(row0, tap, 0.0)
        b0 = tb_ref[mu] if mu < nblk else jnp.zeros_like(tap)
        return b0 + sgn * xre, jnp.where(row0, b0 + xim, -sgn * xim), b0

    prev = block(-nblk)
    for m in range(-(nblk - 1), nblk):
        cur = block(m)
        hre = cur[0] + sgn * (prev[0] - prev[2])
        him = cur[1] + sgn * (prev[1] - jnp.where(row0, prev[2], 0.0))
        lag = m + nblk - 1
        p_ref[lag] = hre.astype(p_ref.dtype)
        q_ref[lag] = jnp.where(row0, 0.0, him).astype(q_ref.dtype)
        r_ref[lag] = jnp.where(row0, him, hre).astype(r_ref.dtype)
        prev = cur


def filter_spectra(filt, fwd, sign, nblk, tb, tf=256, tn=256):
    d = D_MODEL
    tf = min(tf, tb)
    ncol = filt.shape[1]
    fb = filt.reshape(nblk, tb, ncol)
    e = block_dft(fwd, fb, 0, ncol, BF16, name="filter_dft").reshape(nblk, 2, tb, ncol)
    taps = fb[:, 0:1, :].astype(F32)
    nlag = 2 * nblk - 1
    per = d // tn
    espec = lambda direction: pl.BlockSpec(
        (nblk, 2, tf, tn), lambda o, fi, ci: (0, 0, fi, (2 * o + direction) * per + ci))
    tspec = lambda direction: pl.BlockSpec(
        (nblk, 1, tn), lambda o, fi, ci: (0, 0, (2 * o + direction) * per + ci))
    ospec = pl.BlockSpec((nlag, tf, tn), lambda o, fi, ci: (0, fi, o * per + ci))
    oshape = jax.ShapeDtypeStruct((nlag, tb, HYENA_ORDER * d), BF16)
    return pl.pallas_call(
        _fspec_kernel,
        grid=(HYENA_ORDER, tb // tf, per),
        in_specs=[espec(0), espec(1), tspec(0), tspec(1), pl.BlockSpec((tf, 1), lambda o, fi, ci: (fi, 0))],
        out_specs=[ospec, ospec, ospec],
        out_shape=[oshape, oshape, oshape],
        compiler_params=_cparams("arbitrary", "arbitrary", "arbitrary"),
        name="filter_spectra",
    )(e, e, taps, taps, sign.reshape(tb, 1))


def kernel(x, c, positions, ada_w, ada_b, norm_mix_g, norm_ffn_g, mla_w_down, mla_q_a_g, mla_kv_a_g, mla_w_uq, mla_w_ukv, mla_q_norm_g, mla_k_norm_g, mla_w_o, hy_w_in, hy_b_in, hy_conv_w, hy_conv_b, hy_f_w1, hy_f_b1, hy_f_freq1, hy_f_w2, hy_f_b2, hy_f_freq2, hy_f_w3, hy_f_b3, hy_f_freq3, hy_f_w4, hy_filt_bias, hy_w_out, moe_wg, moe_bg, moe_we, moe_be, moe_w_gate, moe_w_up, moe_w_down):
    batch, seq, d = x.shape
    t = batch * seq
    hh = MLA_HEADS
    xf = x.reshape(t, d)

    mod = adaln(c, ada_w, ada_b)

    def mods(i):
        return [mod[i, :, j * d:(j + 1) * d] for j in range(6)]

    def moe_layer(a, w_o, xin, g1, i, sh2, sc2, g2):
        w_r = jnp.concatenate([moe_we[i], moe_wg[i], jnp.zeros((d, ROUTER_LANES - N_EXPERTS - N_GROUPS), F32)], axis=1)
        w_hi = w_r.astype(BF16)
        w_rem = w_r - w_hi.astype(F32)
        w_mid = w_rem.astype(BF16)
        w_lo = (w_rem - w_mid.astype(F32)).astype(BF16)
        w_r = jnp.concatenate([w_hi, w_mid, w_lo, jnp.zeros((d, LANES - 3 * ROUTER_LANES), BF16)], axis=1)
        b_r = jnp.concatenate([moe_be[i], moe_bg[i], jnp.zeros((LANES - N_EXPERTS - N_GROUPS,), F32)]).reshape(1, LANES)
        tm = min(MOE_TILE, seq)
        xn, h_ext, meta, counts = moe_router(a, w_o.astype(BF16), xin, g1, norm_ffn_g[i], sh2, sc2, w_r, b_r, seq, tm)
        return moe_experts(h_ext, meta, counts, moe_w_gate[i].astype(BF16), moe_w_up[i].astype(BF16),
                           moe_w_down[i].astype(BF16), xn, g2, seq, tm)

    sh1, sc1, g1, sh2, sc2, g2 = mods(0)
    lat_w = mla_w_down[0]
    w_dn = jnp.concatenate([lat_w, lat_w[:, Q_LORA + KV_LORA:]], axis=1).astype(BF16)

    inv_freq = 1.0 / (ROPE_THETA ** (jnp.arange(0, QK_ROPE, 2, dtype=F32) / QK_ROPE))
    zeros_half = jnp.zeros((LANES - QK_ROPE,), F32)
    ones_half = jnp.ones((ROPE_HALF,), F32)
    freq_lanes = jnp.concatenate([inv_freq, inv_freq, zeros_half])
    ang = positions.reshape(t, 1).astype(F32) * freq_lanes[None, :]
    cos_t = jnp.cos(ang) * jnp.concatenate([ones_half, ones_half, zeros_half])[None, :]
    sin_t = jnp.sin(ang) * jnp.concatenate([-ones_half, ones_half, zeros_half])[None, :]

    wq = mla_w_uq[0].reshape(Q_LORA, hh, QK_HEAD)
    wq = jnp.concatenate([wq, wq[:, :, QK_NOPE:]], axis=-1)
    wq = wq.reshape(Q_LORA, hh * QK_PAD).astype(BF16)
    wkv = mla_w_ukv[0].reshape(KV_LORA, hh, QK_NOPE + V_HEAD)
    wk = wkv[:, :, :QK_NOPE].reshape(KV_LORA, hh * QK_NOPE).astype(BF16)
    wv = wkv[:, :, QK_NOPE:].reshape(KV_LORA, hh * V_HEAD).astype(BF16)
    gq = jnp.concatenate([mla_q_norm_g[0], mla_q_norm_g[0][QK_NOPE:]]).reshape(1, QK_PAD)
    gkn = mla_k_norm_g[0][:QK_NOPE].reshape(1, LANES)
    gkp = jnp.concatenate([mla_k_norm_g[0][QK_NOPE:], mla_k_norm_g[0][QK_NOPE:]]).reshape(1, LANES)
    q, k, v = mla_qkv(xf, norm_mix_g[0], sh1, sc1, w_dn, cos_t, sin_t, mla_q_a_g[0].reshape(1, Q_LORA), mla_kv_a_g[0].reshape(1, KV_LORA),
                      wq, wk, wv, gq, gkn, gkp, batch, seq)
    o = attention(q, k, v)
    xf = moe_layer(o.reshape(t, hh * V_HEAD), mla_w_o[0], xf, g1, 0, sh2, sc2, g2)

    sh1, sc1, g1, sh2, sc2, g2 = mods(1)
    nblk = CONV_BLOCKS
    tb = seq // nblk
    fwd, inv, sign = _dft_matrices(tb)

    tt = jnp.linspace(0.0, 1.0, seq, dtype=F32)[:, None]
    wfreq = 2.0 * math.pi * jnp.arange(seq, dtype=F32)[:, None] / seq
    fr = jnp.linspace(1e-4, FILTER_BANDS - 1, FILTER_BANDS, dtype=F32)[None, :]
    z = jnp.concatenate([tt, jnp.cos(fr * wfreq), -jnp.sin(fr * wfreq)], axis=-1)
    deltas = jnp.abs(jnp.linspace(math.log(FAST_DECAY) / DECAY_TARGET, math.log(SLOW_DECAY) / DECAY_TARGET, d, dtype=F32))
    filt = hyena_filter(z, hy_f_w1[0], hy_f_b1[0], hy_f_freq1[0], hy_f_w2[0], hy_f_b2[0], hy_f_freq2[0],
                        hy_f_w3[0], hy_f_b3[0], hy_f_freq3[0], hy_f_w4[0], jnp.tile(deltas, HYENA_ORDER * 2))
    p, qc, r = filter_spectra(filt, fwd, sign, nblk, tb)

    u = norm_mod_matmul(xf, norm_mix_g[1], sh1, sc1, hy_w_in[0].astype(BF16), hy_b_in[0], seq, BF16, name="hy_in")
    u3 = u.reshape(batch, seq, 3 * d)
    cw3 = hy_conv_w[0].reshape(3, 3, d)
    cb3 = hy_conv_b[0].reshape(3, 1, d)

    zsrc, zcol = u3, 2 * d
    for order in range(HYENA_ORDER):
        cw = jnp.stack([cw3[:, 2], cw3[:, order]], axis=0)
        cb = jnp.stack([cb3[2], cb3[order]], axis=0)
        zsrc = hyena_conv(fwd, inv, zsrc, zcol, order == 0, u3, order * d, cw, cb, p, qc, r, order * d,
                          hy_filt_bias[0, order], nblk, name=f"hyena_conv{order}")
        zcol = 0
    xf = moe_layer(zsrc.reshape(t, d), hy_w_out[0], xf, g1, 1, sh2, sc2, g2)
    return xf.reshape(batch, seq, d)
```
